```python
import jax, jax.numpy as jnp
from jax import lax
import numpy as np

D_MODEL = 2048
BATCH = 2
SEQ = 16384
DEPTH = 4
DEC_BATCH = 16
DEC_SEQ = 32
PAST_LEN = 4096

CHUNK = 64
HEAD_SIZE = 64
N_HEADS = D_MODEL // HEAD_SIZE
DECAY_LORA = 96
AAA_LORA = 96
GATE_LORA = 256
N_SHIFT = 6
CONV_WIDTH = 31
CONV_STATE = CONV_WIDTH - 1
D_FF = 4 * D_MODEL
N_MIXERS = 2
N_RWKV = (DEPTH + 1) // 2
N_CONV = DEPTH // 2
ALPHA = (2.0 * DEPTH) ** 0.25
BETA = (8.0 * DEPTH) ** -0.25
LN_EPS = 1e-5
GN_EPS = 64e-5

kernel_name = "rwkv7_conformer_conv_hybrid_stream_step"


def layer_norm(x, g, b, eps=LN_EPS):
    xf = x.astype(jnp.float32)
    mu = jnp.mean(xf, axis=-1, keepdims=True)
    var = jnp.mean(jnp.square(xf - mu), axis=-1, keepdims=True)
    return ((xf - mu) * lax.rsqrt(var + eps) * g + b).astype(x.dtype)


def wkv7_scan(s0, r, decay, k, v, kk, a):
    def step(s, inp):
        r_t, w_t, k_t, v_t, kk_t, a_t = inp
        sa = jnp.einsum('bhij,bhj->bhi', s, kk_t)
        s = (s * w_t[:, :, None, :]
             - sa[..., None] * (kk_t * a_t)[:, :, None, :]
             + v_t[..., None] * k_t[:, :, None, :])
        y_t = jnp.einsum('bhij,bhj->bhi', s, r_t)
        return s, y_t
    xs = tuple(jnp.moveaxis(t, 1, 0) for t in (r, decay, k, v, kk, a))
    s_last, ys = lax.scan(step, s0, xs)
    return s_last, jnp.moveaxis(ys, 0, 1)


def rwkv7_mix(x, shift_state, wkv_state, mu, w_r, w_k, w_v, w_o, w0, w1, w2,
              a0, a1, a2, g1, g2, k_k, k_a, r_k, gn_g, gn_b):
    B, T, D = x.shape
    x_prev = jnp.concatenate([shift_state[:, None, :].astype(x.dtype), x[:, :-1]], axis=1)
    xx = x_prev - x
    xr, xw, xk, xv, xa, xg = [x + xx * mu[i] for i in range(N_SHIFT)]
    r = xr @ w_r
    k = xk @ w_k
    v = xv @ w_v
    w_log = -jax.nn.softplus(-(w0 + jnp.tanh(xw @ w1) @ w2)) - 0.5
    a = jax.nn.sigmoid(a0 + (xa @ a1) @ a2)
    g = jax.nn.sigmoid(xg @ g1) @ g2

    def heads(t):
        return t.reshape(B, T, N_HEADS, HEAD_SIZE).astype(jnp.float32)

    kk = heads(k * k_k)
    kk = kk * lax.rsqrt(jnp.maximum(jnp.sum(kk * kk, axis=-1, keepdims=True), 1e-24))
    k = k * (1 + (a - 1) * k_a)
    r_h, k_h, v_h, a_h = heads(r), heads(k), heads(v), heads(a)
    decay = jnp.exp(-jnp.exp(heads(w_log)))
    s_last, y = wkv7_scan(wkv_state.astype(jnp.float32), r_h, decay, k_h, v_h, kk, a_h)
    m = jnp.mean(y, axis=-1, keepdims=True)
    var = jnp.mean(jnp.square(y - m), axis=-1, keepdims=True)
    y = ((y - m) * lax.rsqrt(var + GN_EPS)).reshape(B, T, D) * gn_g + gn_b
    bonus = (jnp.sum(r_h * k_h * r_k.astype(jnp.float32), axis=-1, keepdims=True) * v_h).reshape(B, T, D)
    out = ((y + bonus) * g).astype(x.dtype) @ w_o
    return out, x[:, -1], s_last.astype(wkv_state.dtype)


def conformer_conv_mix(x, conv_state, w_in, b_in, dw, dw_b, cln_g, cln_b, w_out, b_out):
    h = x @ w_in + b_in
    u = h[..., :D_MODEL] * jax.nn.sigmoid(h[..., D_MODEL:])
    up = jnp.concatenate([conv_state.astype(u.dtype), u], axis=1)
    c = lax.conv_general_dilated(up, dw[:, None, :].astype(up.dtype), window_strides=(1,),
                                 padding='VALID', dimension_numbers=('NWC', 'WIO', 'NWC'),
                                 feature_group_count=D_MODEL) + dw_b
    c = jax.nn.silu(layer_norm(c, cln_g, cln_b))
    out = c @ w_out + b_out
    return out, up[:, -CONV_STATE:]


def sq_relu_mlp(x, w_up, w_down):
    return jnp.square(jax.nn.relu(x @ w_up)) @ w_down


def run_trunk(x, st_shift, st_wkv, st_conv, p):
    new_shift, new_wkv, new_conv = [], [], []
    for i in range(DEPTH):
        j = i // N_MIXERS
        if i % N_MIXERS == 0:
            m, sh, s = rwkv7_mix(x, st_shift[j], st_wkv[j], p['mu'][j], p['w_r'][j], p['w_k'][j],
                                 p['w_v'][j], p['w_o'][j], p['w0'][j], p['w1'][j], p['w2'][j],
                                 p['a0'][j], p['a1'][j], p['a2'][j], p['g1'][j], p['g2'][j],
                                 p['k_k'][j], p['k_a'][j], p['r_k'][j], p['gn_g'][j], p['gn_b'][j])
            new_shift.append(sh)
            new_wkv.append(s)
        else:
            m, cs = conformer_conv_mix(x, st_conv[j], p['c_w_in'][j], p['c_b_in'][j], p['c_dw'][j],
                                       p['c_dw_b'][j], p['c_ln_g'][j], p['c_ln_b'][j],
                                       p['c_w_out'][j], p['c_b_out'][j])
            new_conv.append(cs)
        x = layer_norm(ALPHA * x + m, p['ln_mix_g'][i], p['ln_mix_b'][i])
        x = layer_norm(ALPHA * x + sq_relu_mlp(x, p['w_up'][i], p['w_down'][i]),
                       p['ln_ffn_g'][i], p['ln_ffn_b'][i])
    return x, jnp.stack(new_wkv), jnp.stack(new_shift), jnp.stack(new_conv)


def setup_inputs(seed: int = 0) -> dict:
    key = jax.random.key(seed)
    ks = iter(jax.random.split(key, 48))
    f32 = jnp.float32

    def nrm(shape, s):
        return jax.random.normal(next(ks), shape, f32) * s

    def unif(shape, lo, hi):
        return jax.random.uniform(next(ks), shape, f32, lo, hi)

    D = D_MODEL
    return {
        'x_prompt': nrm((BATCH, SEQ, D), 1.0),
        'x_sample': nrm((DEC_BATCH, DEC_SEQ, D), 1.0),
        'state_wkv': nrm((N_RWKV, DEC_BATCH, N_HEADS, HEAD_SIZE, HEAD_SIZE), 0.1),
        'state_shift': nrm((N_RWKV, DEC_BATCH, D), 1.0),
        'state_conv': nrm((N_CONV, DEC_BATCH, CONV_STATE, D), 0.5),
        'mu': unif((N_RWKV, N_SHIFT, D), 0.0, 1.0),
        'w_r': nrm((N_RWKV, D, D), D ** -0.5),
        'w_k': nrm((N_RWKV, D, D), D ** -0.5),
        'w_v': nrm((N_RWKV, D, D), BETA * D ** -0.5),
        'w_o': nrm((N_RWKV, D, D), BETA * D ** -0.5),
        'w0': unif((N_RWKV, D), -3.0, 3.0),
        'w1': nrm((N_RWKV, D, DECAY_LORA), D ** -0.5),
        'w2': nrm((N_RWKV, DECAY_LORA, D), 0.3 * DECAY_LORA ** -0.5),
        'a0': nrm((N_RWKV, D), 0.1),
        'a1': nrm((N_RWKV, D, AAA_LORA), D ** -0.5),
        'a2': nrm((N_RWKV, AAA_LORA, D), 0.3 * AAA_LORA ** -0.5),
        'g1': nrm((N_RWKV, D, GATE_LORA), D ** -0.5),
        'g2': nrm((N_RWKV, GATE_LORA, D), GATE_LORA ** -0.5),
        'k_k': 0.85 + nrm((N_RWKV, D), 0.1),
        'k_a': 1.0 + nrm((N_RWKV, D), 0.1),
        'r_k': nrm((N_RWKV, N_HEADS, HEAD_SIZE), 0.1),
        'gn_g': 1.0 + nrm((N_RWKV, D), 0.01),
        'gn_b': nrm((N_RWKV, D), 0.01),
        'c_w_in': nrm((N_CONV, D, 2 * D), D ** -0.5),
        'c_b_in': nrm((N_CONV, 2 * D), 0.01),
        'c_dw': nrm((N_CONV, CONV_WIDTH, D), CONV_WIDTH ** -0.5),
        'c_dw_b': nrm((N_CONV, D), 0.01),
        'c_ln_g': 1.0 + nrm((N_CONV, D), 0.01),
        'c_ln_b': nrm((N_CONV, D), 0.01),
        'c_w_out': nrm((N_CONV, D, D), BETA * D ** -0.5),
        'c_b_out': nrm((N_CONV, D), 0.01),
        'w_up': nrm((DEPTH, D, D_FF), BETA * D ** -0.5),
        'w_down': nrm((DEPTH, D_FF, D), BETA * D_FF ** -0.5),
        'ln_mix_g': 1.0 + nrm((DEPTH, D), 0.01),
        'ln_mix_b': nrm((DEPTH, D), 0.01),
        'ln_ffn_g': 1.0 + nrm((DEPTH, D), 0.01),
        'ln_ffn_b': nrm((DEPTH, D), 0.01),
    }


def reference(x_prompt, x_sample, state_wkv, state_shift, state_conv,
              mu, w_r, w_k, w_v, w_o, w0, w1, w2, a0, a1, a2, g1, g2, k_k, k_a, r_k, gn_g, gn_b,
              c_w_in, c_b_in, c_dw, c_dw_b, c_ln_g, c_ln_b, c_w_out, c_b_out,
              w_up, w_down, ln_mix_g, ln_mix_b, ln_ffn_g, ln_ffn_b):
    p = {'mu': mu, 'w_r': w_r, 'w_k': w_k, 'w_v': w_v, 'w_o': w_o, 'w0': w0, 'w1': w1, 'w2': w2,
         'a0': a0, 'a1': a1, 'a2': a2, 'g1': g1, 'g2': g2, 'k_k': k_k, 'k_a': k_a, 'r_k': r_k,
         'gn_g': gn_g, 'gn_b': gn_b, 'c_w_in': c_w_in, 'c_b_in': c_b_in, 'c_dw': c_dw,
         'c_dw_b': c_dw_b, 'c_ln_g': c_ln_g, 'c_ln_b': c_ln_b, 'c_w_out': c_w_out,
         'c_b_out': c_b_out, 'w_up': w_up, 'w_down': w_down, 'ln_mix_g': ln_mix_g,
         'ln_mix_b': ln_mix_b, 'ln_ffn_g': ln_ffn_g, 'ln_ffn_b': ln_ffn_b}
    bp = x_prompt.shape[0]
    z_wkv = jnp.zeros((N_RWKV, bp, N_HEADS, HEAD_SIZE, HEAD_SIZE), x_prompt.dtype)
    z_shift = jnp.zeros((N_RWKV, bp, D_MODEL), x_prompt.dtype)
    z_conv = jnp.zeros((N_CONV, bp, CONV_STATE, D_MODEL), x_prompt.dtype)
    y_prompt, wkv_p, shift_p, conv_p = run_trunk(x_prompt, z_shift, z_wkv, z_conv, p)
    y_sample, wkv_s, shift_s, conv_s = run_trunk(x_sample, state_shift, state_wkv, state_conv, p)
    return (y_prompt, y_sample, wkv_p, shift_p, conv_p, wkv_s, shift_s, conv_s)
```

```python
import functools
import math

import jax
import jax.numpy as jnp
from jax import lax
from jax.experimental import pallas as pl
from jax.experimental.pallas import tpu as pltpu

F32 = jnp.float32
BF16 = jnp.bfloat16

HEAD = 64
GROUP = 4
GW = HEAD * GROUP
CHUNK = 64
CONV_W = 31
CONV_STATE = CONV_W - 1
HALO = 32
LN_EPS = 1e-5
GN_EPS = 64e-5
VMEM_LIMIT = 56 * 1024 * 1024

NN = ((1,), (0,))
NT = ((1,), (1,))


def _cparams(sem):
    return pltpu.CompilerParams(dimension_semantics=sem, vmem_limit_bytes=VMEM_LIMIT)


def _dot(a, b, dims=NN):
    return lax.dot_general(a, b, (dims, ((), ())), preferred_element_type=F32)


def _split(x):
    hi = x.astype(BF16)
    lo = (x - hi.astype(F32)).astype(BF16)
    return hi, lo


def _mm3(a, b, dims=NN):
    return _dot(a[0], b[0], dims) + (_dot(a[0], b[1], dims) + _dot(a[1], b[0], dims))


def _layer_norm(x, g, b):
    mu = jnp.mean(x, axis=-1, keepdims=True)
    xc = x - mu
    var = jnp.mean(xc * xc, axis=-1, keepdims=True)
    return xc * lax.rsqrt(var + LN_EPS) * g + b


def _sigmoid(x):
    return 1.0 / (1.0 + jnp.exp(-x))


def _lane_head(shape, dim):
    return lax.broadcasted_iota(jnp.int32, shape, dim) // HEAD


def _block_diag(x):
    head = _lane_head(x.shape, 1)
    return jnp.concatenate([jnp.where(head == h, x, 0.0) for h in range(GROUP)], axis=0)


def _head_ones():
    return jnp.where(_lane_head((GW, GW), 0) == _lane_head((GW, GW), 1), 1.0, 0.0).astype(BF16)


def _head_sum(x, ones):
    hi, lo = _split(x)
    return _dot(hi, ones) + _dot(lo, ones)


def _mlp_kernel(x_ref, wu_ref, wd_ref, g_ref, b_ref, o_ref, acc_ref, xb_ref, *, alpha):
    j = pl.program_id(1)

    @pl.when(j == 0)
    def _():
        xb_ref[...] = x_ref[...].astype(BF16)
        acc_ref[...] = jnp.zeros_like(acc_ref)

    h = jnp.maximum(_dot(xb_ref[...], wu_ref[...]), 0.0)
    acc_ref[...] += _dot((h * h).astype(BF16), wd_ref[...])

    @pl.when(j == pl.num_programs(1) - 1)
    def _():
        o_ref[...] = _layer_norm(alpha * x_ref[...] + acc_ref[...], g_ref[...], b_ref[...])


def _mlp(x, w_up, w_down, g, b, alpha, tm, tf):
    rows, d = x.shape
    dff = w_up.shape[1]
    return pl.pallas_call(
        functools.partial(_mlp_kernel, alpha=alpha),
        grid=(rows // tm, dff // tf),
        in_specs=[
            pl.BlockSpec((tm, d), lambda i, j: (i, 0)),
            pl.BlockSpec((d, tf), lambda i, j: (0, j)),
            pl.BlockSpec((tf, d), lambda i, j: (j, 0)),
            pl.BlockSpec((1, d), lambda i, j: (0, 0)),
            pl.BlockSpec((1, d), lambda i, j: (0, 0)),
        ],
        out_specs=pl.BlockSpec((tm, d), lambda i, j: (i, 0)),
        out_shape=jax.ShapeDtypeStruct((rows, d), F32),
        scratch_shapes=[pltpu.VMEM((tm, d), F32), pltpu.VMEM((tm, d), BF16)],
        compiler_params=_cparams(("parallel", "arbitrary")),
        name="mlp",
    )(x, w_up, w_down, g, b)


def _rwkv_proj_kernel(x_ref, prev_ref, mu_ref, w1_ref, a1_ref, g1_ref,
                      wr_ref, wk_ref, wv_ref, w2_ref, a2_ref, g2_ref,
                      w0_ref, a0_ref, kk_ref, ka_ref,
                      r_o, lw_o, k_o, v_o, kap_o, b_o, g_o,
                      xm_ref, hw_ref, ha_ref, hg_ref):
    j = pl.program_id(2)

    @pl.when(j == 0)
    def _():
        x = x_ref[0]
        rolled = pltpu.roll(x, 1, axis=0)
        first = lax.broadcasted_iota(jnp.int32, x.shape, 0) == 0
        xx = jnp.where(first, prev_ref[0, 0], rolled) - x
        mu = mu_ref[...]
        xm_ref[0] = (x + xx * mu[0:1]).astype(BF16)
        xm_ref[1] = (x + xx * mu[2:3]).astype(BF16)
        xm_ref[2] = (x + xx * mu[3:4]).astype(BF16)
        xw = (x + xx * mu[1:2]).astype(BF16)
        xa = (x + xx * mu[4:5]).astype(BF16)
        xg = (x + xx * mu[5:6]).astype(BF16)
        hw_ref[...] = jnp.tanh(_dot(xw, w1_ref[...])).astype(BF16)
        ha_ref[...] = _dot(xa, a1_ref[...]).astype(BF16)
        hg_ref[...] = _sigmoid(_dot(xg, g1_ref[...])).astype(BF16)

    r = _dot(xm_ref[0], wr_ref[...])
    k = _dot(xm_ref[1], wk_ref[...])
    v = _dot(xm_ref[2], wv_ref[...])
    z = -(w0_ref[...] + _dot(hw_ref[...], w2_ref[...]))
    w_log = -(jnp.maximum(z, 0.0) + jnp.log(1.0 + jnp.exp(-jnp.abs(z)))) - 0.5
    a = _sigmoid(a0_ref[...] + _dot(ha_ref[...], a2_ref[...]))
    g = _dot(hg_ref[...], g2_ref[...])

    kk = k * kk_ref[...]
    ones = _head_ones()
    sq = kk * kk
    ss = jnp.concatenate([_head_sum(sq[:, s:s + GW], ones) for s in range(0, sq.shape[1], GW)], axis=1)
    kap = kk * lax.rsqrt(jnp.maximum(ss, 1e-24))

    r_o[0] = r
    lw_o[0] = -jnp.exp(w_log)
    k_o[0] = k * (1.0 + (a - 1.0) * ka_ref[...])
    v_o[0] = v
    kap_o[0] = kap
    b_o[0] = kap * a
    g_o[0] = g


def _rwkv_proj(x, prev, p, tm, tn):
    bsz, t, d = x.shape
    nt = t // tm
    lw, la, lg = p['w1'].shape[1], p['a1'].shape[1], p['g1'].shape[1]
    full = lambda shape: pl.BlockSpec(shape, lambda b, i, j: (0,) * len(shape))
    col = lambda rows: pl.BlockSpec((rows, tn), lambda b, i, j: (0, j))
    out_spec = pl.BlockSpec((1, tm, tn), lambda b, i, j: (b, i, j))
    out_sds = jax.ShapeDtypeStruct((bsz, t, d), F32)
    return pl.pallas_call(
        _rwkv_proj_kernel,
        grid=(bsz, nt, d // tn),
        in_specs=[
            pl.BlockSpec((1, tm, d), lambda b, i, j: (b, i, 0)),
            pl.BlockSpec((1, 1, 1, d), lambda b, i, j: (b, i, 0, 0)),
            full((8, d)), full((d, lw)), full((d, la)), full((d, lg)),
            col(d), col(d), col(d), col(lw), col(la), col(lg),
            col(1), col(1), col(1), col(1),
        ],
        out_specs=[out_spec] * 7,
        out_shape=[out_sds] * 7,
        scratch_shapes=[pltpu.VMEM((3, tm, d), BF16), pltpu.VMEM((tm, lw), BF16),
                        pltpu.VMEM((tm, la), BF16), pltpu.VMEM((tm, lg), BF16)],
        compiler_params=_cparams(("parallel", "parallel", "arbitrary")),
        name="rwkv_proj",
    )(x, prev, p['mu'], p['w1'], p['a1'], p['g1'], p['w_r'], p['w_k'], p['w_v'],
      p['w2'], p['a2'], p['g2'], p['w0'], p['a0'], p['k_k'], p['k_a'])


def _extract_heads(x):
    head = _lane_head((HEAD, GW), 1)
    out = jnp.zeros((HEAD, GW), F32)
    for h in range(GROUP):
        out = jnp.where(head == h, x[h * HEAD:(h + 1) * HEAD, :], out)
    return out


def _wkv_chunk(r, lw, k, v, kap, b, rk, ones):
    c = CHUNK
    row = lax.broadcasted_iota(jnp.int32, (c, GW), 0)
    pos = lax.broadcasted_iota(jnp.int32, (c, GW), 1) % HEAD
    tri = jnp.where(lax.broadcasted_iota(jnp.int32, (c, c), 1) <= lax.broadcasted_iota(jnp.int32, (c, c), 0),
                    1.0, 0.0).astype(BF16)

    lw_hi = lw.astype(BF16)
    rem = lw - lw_hi.astype(F32)
    lw_mid = rem.astype(BF16)
    lw_lo = (rem - lw_mid.astype(F32)).astype(BF16)
    cum = _dot(tri, lw_hi) + (_dot(tri, lw_mid) + _dot(tri, lw_lo))
    end = cum[c - 1:c, :]

    e_neg = jnp.exp(-cum)
    e_end = jnp.exp(end - cum)
    rt = r * jnp.exp(cum)
    kq = kap * jnp.exp(cum - lw)
    bh = b * e_neg
    kh = k * e_neg
    bp = b * e_end
    kp = k * e_end

    lhs = _split(jnp.concatenate([kq, rt], axis=0))
    gb = _mm3(lhs, _split(_block_diag(bh)), NT)
    gk = _mm3(lhs, _split(_block_diag(kh)), NT)
    strict = pos < row
    lower = pos <= row
    l_ub = jnp.where(strict, gb[:c], 0.0)
    a_uk = jnp.where(strict, gk[:c], 0.0)
    a_rb = jnp.where(lower, gb[c:], 0.0)
    a_rk = jnp.where(lower, gk[c:], 0.0)

    tm = jnp.where(pos == row, 1.0, 0.0) - l_ub
    pw = l_ub
    for _ in range(int(math.log2(c)) - 1):
        pw = _mm3(_split(pw), _split(_block_diag(pw)))
        tm = tm + _mm3(_split(tm), _split(_block_diag(pw)))
    tsp = _split(tm)

    vbd = _split(_block_diag(v))
    w = _mm3(_split(a_uk), vbd)
    p = _mm3(tsp, _split(_block_diag(kq)))
    u0 = -_mm3(tsp, _split(_block_diag(w)))

    a_rb16 = a_rb.astype(BF16)
    qp = rt - _dot(a_rb16, _block_diag(p).astype(BF16))
    y0 = _dot(a_rb16, _block_diag(u0).astype(BF16)) + _dot(a_rk.astype(BF16), vbd[0])

    bpt = _split(bp.T)
    kpt = _split(kp.T)
    m_full = _mm3(bpt, _split(p))
    d_full = _mm3(bpt, _split(u0)) + _mm3(kpt, _split(v))
    decay = jnp.where(pos == row, jnp.exp(end), 0.0)
    m_cat = decay - _extract_heads(m_full)
    d_cat = _extract_heads(d_full)

    bonus = _head_sum(r * k * rk, ones) * v
    return qp, y0, m_cat, d_cat, bonus


def _wkv_pre_kernel(r_ref, lw_ref, k_ref, v_ref, kap_ref, b_ref, rk_ref,
                    qp_o, y0_o, m_o, d_o, bonus_o, *, nchunk):
    ones = _head_ones()
    rk = rk_ref[...]

    def body(ci, carry):
        rows = pl.ds(pl.multiple_of(ci * CHUNK, CHUNK), CHUNK)
        qp, y0, m_cat, d_cat, bonus = _wkv_chunk(
            r_ref[0, rows, :], lw_ref[0, rows, :], k_ref[0, rows, :], v_ref[0, rows, :],
            kap_ref[0, rows, :], b_ref[0, rows, :], rk, ones)
        qp_o[0, rows, :] = qp
        y0_o[0, rows, :] = y0
        bonus_o[0, rows, :] = bonus
        m_o[0, ci] = m_cat
        d_o[0, ci] = d_cat
        return carry

    lax.fori_loop(0, nchunk, body, 0)


def _wkv_pre(r, lw, k, v, kap, b, rk, tc):
    bsz, t, d = r.shape
    nc = t // CHUNK
    seq = pl.BlockSpec((1, tc * CHUNK, GW), lambda bi, ci, gi: (bi, ci, gi))
    mat = pl.BlockSpec((1, tc, HEAD, GW), lambda bi, ci, gi: (bi, ci, 0, gi))
    seq_sds = jax.ShapeDtypeStruct((bsz, t, d), F32)
    mat_sds = jax.ShapeDtypeStruct((bsz, nc, HEAD, d), F32)
    return pl.pallas_call(
        functools.partial(_wkv_pre_kernel, nchunk=tc),
        grid=(bsz, nc // tc, d // GW),
        in_specs=[seq] * 6 + [pl.BlockSpec((1, GW), lambda bi, ci, gi: (0, gi))],
        out_specs=[seq, seq, mat, mat, seq],
        out_shape=[seq_sds, seq_sds, mat_sds, mat_sds, seq_sds],
        compiler_params=_cparams(("parallel", "parallel", "parallel")),
        name="wkv_pre",
    )(r, lw, k, v, kap, b, rk)


def _wkv_seq_kernel(qp_ref, y0_ref, m_ref, d_ref, h0_ref, y_o, h_o, h_ref, *, nb, nchunk, ngroup):
    ci = pl.program_id(1)

    @pl.when(ci == 0)
    def _():
        h_ref[...] = h0_ref[...]

    def body(c, carry):
        rows = pl.ds(pl.multiple_of(c * CHUNK, CHUNK), CHUNK)
        for bi in range(nb):
            for gi in range(ngroup):
                lanes = slice(gi * GW, (gi + 1) * GW)
                hbd = _split(_block_diag(h_ref[bi, :, lanes]))
                lhs = _split(jnp.concatenate([qp_ref[bi, rows, lanes], m_ref[bi, c, :, lanes]], axis=0))
                res = _mm3(lhs, hbd)
                y_o[bi, rows, lanes] = res[:CHUNK] + y0_ref[bi, rows, lanes]
                h_ref[bi, :, lanes] = res[CHUNK:] + d_ref[bi, c, :, lanes]
        return carry

    lax.fori_loop(0, nchunk, body, 0)

    @pl.when(ci == pl.num_programs(1) - 1)
    def _():
        h_o[...] = h_ref[...]


def _wkv_seq(qp, y0, m, dmat, h0, nb, tc):
    bsz, t, d = qp.shape
    nc = t // CHUNK
    seq = pl.BlockSpec((nb, tc * CHUNK, d), lambda bi, ci: (bi, ci, 0))
    mat = pl.BlockSpec((nb, tc, HEAD, d), lambda bi, ci: (bi, ci, 0, 0))
    st = pl.BlockSpec((nb, HEAD, d), lambda bi, ci: (bi, 0, 0))
    return pl.pallas_call(
        functools.partial(_wkv_seq_kernel, nb=nb, nchunk=tc, ngroup=d // GW),
        grid=(bsz // nb, nc // tc),
        in_specs=[seq, seq, mat, mat, st],
        out_specs=[seq, st],
        out_shape=[jax.ShapeDtypeStruct((bsz, t, d), F32), jax.ShapeDtypeStruct((bsz, HEAD, d), F32)],
        scratch_shapes=[pltpu.VMEM((nb, HEAD, d), F32)],
        compiler_params=_cparams(("parallel", "arbitrary")),
        name="wkv_seq",
    )(qp, y0, m, dmat, h0)


def _rwkv_out_kernel(y_ref, bonus_ref, g_ref, x_ref, wo_ref, gng_ref, gnb_ref, lng_ref, lnb_ref, o_ref, *, alpha):
    ones = _head_ones()
    y = y_ref[...]
    parts = []
    for s in range(0, y.shape[1], GW):
        ys = y[:, s:s + GW]
        mean = _head_sum(ys, ones) * (1.0 / HEAD)
        yc = ys - mean
        var = _head_sum(yc * yc, ones) * (1.0 / HEAD)
        parts.append(yc * lax.rsqrt(var + GN_EPS))
    yn = jnp.concatenate(parts, axis=1) * gng_ref[...] + gnb_ref[...]
    z = ((yn + bonus_ref[...]) * g_ref[...]).astype(BF16)
    out = _dot(z, wo_ref[...])
    o_ref[...] = _layer_norm(alpha * x_ref[...] + out, lng_ref[...], lnb_ref[...])


def _rwkv_out(y, bonus, g, x, w_o, gn_g, gn_b, ln_g, ln_b, alpha, tm):
    rows, d = x.shape
    tile = pl.BlockSpec((tm, d), lambda i: (i, 0))
    vec = pl.BlockSpec((1, d), lambda i: (0, 0))
    return pl.pallas_call(
        functools.partial(_rwkv_out_kernel, alpha=alpha),
        grid=(rows // tm,),
        in_specs=[tile, tile, tile, tile, pl.BlockSpec((d, d), lambda i: (0, 0)), vec, vec, vec, vec],
        out_specs=tile,
        out_shape=jax.ShapeDtypeStruct((rows, d), F32),
        compiler_params=_cparams(("parallel",)),
        name="rwkv_out",
    )(y, bonus, g, x, w_o, gn_g, gn_b, ln_g, ln_b)


def _glu_kernel(x_ref, wa_ref, wg_ref, ba_ref, bg_ref, u_ref, xb_ref):
    @pl.when(pl.program_id(1) == 0)
    def _():
        xb_ref[...] = x_ref[...].astype(BF16)

    xb = xb_ref[...]
    val = _dot(xb, wa_ref[...]) + ba_ref[...]
    gate = _dot(xb, wg_ref[...]) + bg_ref[...]
    u_ref[...] = val * _sigmoid(gate)


def _glu(x, w_in, b_in, tm, tn):
    rows, d = x.shape
    nj = d // tn
    return pl.pallas_call(
        _glu_kernel,
        grid=(rows // tm, nj),
        in_specs=[
            pl.BlockSpec((tm, d), lambda i, j: (i, 0)),
            pl.BlockSpec((d, tn), lambda i, j: (0, j)),
            pl.BlockSpec((d, tn), lambda i, j: (0, nj + j)),
            pl.BlockSpec((1, tn), lambda i, j: (0, j)),
            pl.BlockSpec((1, tn), lambda i, j: (0, nj + j)),
        ],
        out_specs=pl.BlockSpec((tm, tn), lambda i, j: (i, j)),
        out_shape=jax.ShapeDtypeStruct((rows, d), F32),
        scratch_shapes=[pltpu.VMEM((tm, d), BF16)],
        compiler_params=_cparams(("parallel", "arbitrary")),
        name="conv_glu",
    )(x, w_in, w_in, b_in, b_in)


def _conv_out_kernel(u_ref, halo_ref, st_ref, x_ref, dw_ref, dwb_ref, cg_ref, cb_ref, wo_ref, bo_ref,
                     lng_ref, lnb_ref, o_ref, win_ref, c_ref, *, alpha, tm):
    i = pl.program_id(1)
    d = u_ref.shape[2]

    @pl.when(i == 0)
    def _():
        win_ref[0:HALO, :] = st_ref[0]

    @pl.when(i > 0)
    def _():
        win_ref[0:HALO, :] = halo_ref[0]

    win_ref[HALO:HALO + tm, :] = u_ref[0]

    zrows = tm + 8
    for ls in range(0, d, GW):
        lanes = slice(ls, ls + GW)
        acc = jnp.zeros((tm, GW), F32)
        for s in range(8):
            z = None
            for q in range((HALO + 8) // 8):
                j = 8 * q + s - 2
                if j < 0 or j >= CONV_W:
                    continue
                nrows = tm if s == 0 else zrows
                term = win_ref[8 * q:8 * q + nrows, lanes] * dw_ref[j:j + 1, lanes]
                z = term if z is None else z + term
            if s == 0:
                acc = acc + z
            else:
                acc = acc + pltpu.roll(z, zrows - s, axis=0)[:tm]
        c_ref[:, lanes] = acc

    c = _layer_norm(c_ref[...] + dwb_ref[...], cg_ref[...], cb_ref[...])
    c = (c * _sigmoid(c)).astype(BF16)
    out = _dot(c, wo_ref[...]) + bo_ref[...]
    o_ref[0] = _layer_norm(alpha * x_ref[0] + out, lng_ref[...], lnb_ref[...])


def _conv_out(u, state_pad, x, dw, dw_b, cln_g, cln_b, w_out, b_out, ln_g, ln_b, alpha, tm):
    bsz, t, d = u.shape
    hb = tm // HALO
    tile = pl.BlockSpec((1, tm, d), lambda b, i: (b, i, 0))
    vec = pl.BlockSpec((1, d), lambda b, i: (0, 0))
    return pl.pallas_call(
        functools.partial(_conv_out_kernel, alpha=alpha, tm=tm),
        grid=(bsz, t // tm),
        in_specs=[
            tile,
            pl.BlockSpec((1, HALO, d), lambda b, i: (b, jnp.maximum(i * hb - 1, 0), 0)),
            pl.BlockSpec((1, HALO, d), lambda b, i: (b, 0, 0)),
            tile,
            pl.BlockSpec((HALO, d), lambda b, i: (0, 0)),
            vec, vec, vec,
            pl.BlockSpec((d, d), lambda b, i: (0, 0)),
            vec, vec, vec,
        ],
        out_specs=tile,
        out_shape=jax.ShapeDtypeStruct((bsz, t, d), F32),
        scratch_shapes=[pltpu.VMEM((HALO + tm, d), F32), pltpu.VMEM((tm, d), F32)],
        compiler_params=_cparams(("parallel", "arbitrary")),
        name="conv_out",
    )(u, u, state_pad, x, dw, dw_b, cln_g, cln_b, w_out, b_out, ln_g, ln_b)


def _pad_cols(w, n):
    return jnp.pad(w, ((0, 0), (0, n - w.shape[1])))


def _pad_rows(w, n):
    return jnp.pad(w, ((0, n - w.shape[0]), (0, 0)))


def _round_up(n, m):
    return -(-n // m) * m


def _prep_params(raw):
    depth = raw['w_up'].shape[0]
    n_rwkv = raw['w_r'].shape[0]
    n_conv = raw['c_w_in'].shape[0]
    row = lambda v: v.reshape(1, -1)
    rwkv = []
    for j in range(n_rwkv):
        lw = _round_up(raw['w1'].shape[2], 128)
        la = _round_up(raw['a1'].shape[2], 128)
        rwkv.append(dict(
            mu=_pad_rows(raw['mu'][j], 8),
            w_r=raw['w_r'][j].astype(BF16), w_k=raw['w_k'][j].astype(BF16), w_v=raw['w_v'][j].astype(BF16),
            w_o=raw['w_o'][j].astype(BF16),
            w1=_pad_cols(raw['w1'][j], lw).astype(BF16), w2=_pad_rows(raw['w2'][j], lw).astype(BF16),
            a1=_pad_cols(raw['a1'][j], la).astype(BF16), a2=_pad_rows(raw['a2'][j], la).astype(BF16),
            g1=raw['g1'][j].astype(BF16), g2=raw['g2'][j].astype(BF16),
            w0=row(raw['w0'][j]), a0=row(raw['a0'][j]), k_k=row(raw['k_k'][j]), k_a=row(raw['k_a'][j]),
            r_k=row(raw['r_k'][j]), gn_g=row(raw['gn_g'][j]), gn_b=row(raw['gn_b'][j])))
    conv = []
    for j in range(n_conv):
        conv.append(dict(
            w_in=raw['c_w_in'][j].astype(BF16), b_in=row(raw['c_b_in'][j]),
            dw=_pad_rows(raw['c_dw'][j], HALO), dw_b=row(raw['c_dw_b'][j]),
            ln_g=row(raw['c_ln_g'][j]), ln_b=row(raw['c_ln_b'][j]),
            w_out=raw['c_w_out'][j].astype(BF16), b_out=row(raw['c_b_out'][j])))
    mlp = [dict(w_up=raw['w_up'][i].astype(BF16), w_down=raw['w_down'][i].astype(BF16),
                mix_g=row(raw['ln_mix_g'][i]), mix_b=row(raw['ln_mix_b'][i]),
                ffn_g=row(raw['ln_ffn_g'][i]), ffn_b=row(raw['ln_ffn_b'][i])) for i in range(depth)]
    return rwkv, conv, mlp


def _rwkv_layer(x, shift, wkv, p, lnp, alpha, cfg):
    bsz, t, d = x.shape
    tm = cfg['tm_seq']
    prev = jnp.concatenate([shift[:, None, :], x[:, tm - 1:t - 1:tm, :]], axis=1)[:, :, None, :]
    r, lw, k, v, kap, b, g = _rwkv_proj(x, prev, p, tm, cfg['tn'])
    tp = _round_up(t, CHUNK)
    if tp != t:
        r, lw, k, v, kap, b = [jnp.pad(a, ((0, 0), (0, tp - t), (0, 0))) for a in (r, lw, k, v, kap, b)]
    qp, y0, m, dmat, bonus = _wkv_pre(r, lw, k, v, kap, b, p['r_k'], cfg['tc_pre'])
    nh = d // HEAD
    h0 = jnp.transpose(wkv, (0, 3, 1, 2)).reshape(bsz, HEAD, d)
    y, h1 = _wkv_seq(qp, y0, m, dmat, h0, cfg['nb_seq'], cfg['tc_seq'])
    new_wkv = jnp.transpose(h1.reshape(bsz, HEAD, nh, HEAD), (0, 2, 3, 1))
    rows = bsz * t
    x1 = _rwkv_out(y[:, :t].reshape(rows, d), bonus[:, :t].reshape(rows, d), g.reshape(rows, d),
                   x.reshape(rows, d), p['w_o'], p['gn_g'], p['gn_b'], lnp['mix_g'], lnp['mix_b'],
                   alpha, cfg['tm_out'])
    return x1.reshape(bsz, t, d), x[:, -1], new_wkv


def _conv_layer(x, state, p, lnp, alpha, cfg):
    bsz, t, d = x.shape
    u = _glu(x.reshape(bsz * t, d), p['w_in'], p['b_in'], cfg['tm_rows'], cfg['tn']).reshape(bsz, t, d)
    state_pad = jnp.pad(state, ((0, 0), (HALO - CONV_STATE, 0), (0, 0)))
    x1 = _conv_out(u, state_pad, x, p['dw'], p['dw_b'], p['ln_g'], p['ln_b'], p['w_out'], p['b_out'],
                   lnp['mix_g'], lnp['mix_b'], alpha, cfg['tm_conv'])
    keep = min(t, CONV_STATE)
    new_state = jnp.concatenate([state[:, keep:], u[:, t - keep:]], axis=1)
    return x1, new_state


def _run_trunk(x, st_shift, st_wkv, st_conv, params, cfg):
    rwkv, conv, mlp = params
    depth = len(mlp)
    alpha = (2.0 * depth) ** 0.25
    bsz, t, d = x.shape
    new_shift, new_wkv, new_conv = [], [], []
    for i in range(depth):
        j = i // 2
        if i % 2 == 0:
            x, sh, s = _rwkv_layer(x, st_shift[j], st_wkv[j], rwkv[j], mlp[i], alpha, cfg)
            new_shift.append(sh)
            new_wkv.append(s)
        else:
            x, cs = _conv_layer(x, st_conv[j], conv[j], mlp[i], alpha, cfg)
            new_conv.append(cs)
        x = _mlp(x.reshape(bsz * t, d), mlp[i]['w_up'], mlp[i]['w_down'], mlp[i]['ffn_g'], mlp[i]['ffn_b'],
                 alpha, cfg['tm_rows'], cfg['tf']).reshape(bsz, t, d)
    return x, jnp.stack(new_wkv), jnp.stack(new_shift), jnp.stack(new_conv)


def _config(bsz, t):
    rows = bsz * t
    tm_seq = min(t, 512)
    nchunk = _round_up(t, CHUNK) // CHUNK
    return dict(
        tm_seq=tm_seq, tn=512, tm_rows=min(rows, 512), tf=1024, tm_out=min(rows, 256),
        tm_conv=min(t, 256), tc_pre=min(nchunk, 4), tc_seq=min(nchunk, 2), nb_seq=min(bsz, 2))


def kernel(x_prompt, x_sample, state_wkv, state_shift, state_conv, mu, w_r, w_k, w_v, w_o, w0, w1, w2, a0, a1, a2, g1, g2, k_k, k_a, r_k, gn_g, gn_b, c_w_in, c_b_in, c_dw, c_dw_b, c_ln_g, c_ln_b, c_w_out, c_b_out, w_up, w_down, ln_mix_g, ln_mix_b, ln_ffn_g, ln_ffn_b):
    raw = dict(mu=mu, w_r=w_r, w_k=w_k, w_v=w_v, w_o=w_o, w0=w0, w1=w1, w2=w2, a0=a0, a1=a1, a2=a2,
               g1=g1, g2=g2, k_k=k_k, k_a=k_a, r_k=r_k, gn_g=gn_g, gn_b=gn_b, c_w_in=c_w_in,
               c_b_in=c_b_in, c_dw=c_dw, c_dw_b=c_dw_b, c_ln_g=c_ln_g, c_ln_b=c_ln_b, c_w_out=c_w_out,
               c_b_out=c_b_out, w_up=w_up, w_down=w_down, ln_mix_g=ln_mix_g, ln_mix_b=ln_mix_b,
               ln_ffn_g=ln_ffn_g, ln_ffn_b=ln_ffn_b)
    params = _prep_params(raw)
    bp, tp, d = x_prompt.shape
    bs, ts, _ = x_sample.shape
    n_rwkv, n_conv = w_r.shape[0], c_w_in.shape[0]
    nh = d // HEAD
    z_wkv = jnp.zeros((n_rwkv, bp, nh, HEAD, HEAD), F32)
    z_shift = jnp.zeros((n_rwkv, bp, d), F32)
    z_conv = jnp.zeros((n_conv, bp, CONV_STATE, d), F32)
    y_p, wkv_p, shift_p, conv_p = _run_trunk(x_prompt, z_shift, z_wkv, z_conv, params, _config(bp, tp))
    y_s, wkv_s, shift_s, conv_s = _run_trunk(x_sample, state_shift, state_wkv, state_conv, params, _config(bs, ts))
    return (y_p, y_s, wkv_p, shift_p, conv_p, wkv_s, shift_s, conv_s)
```

```python
import functools
import math

import jax
import jax.numpy as jnp
from jax import lax
from jax.experimental import pallas as pl
from jax.experimental.pallas import tpu as pltpu

F32 = jnp.float32
BF16 = jnp.bfloat16

HEAD = 64
GROUP = 4
GW = HEAD * GROUP
CHUNK = 64
CONV_W = 31
CONV_STATE = CONV_W - 1
HALO = 32
LN_EPS = 1e-5
GN_EPS = 64e-5
VMEM_LIMIT = 56 * 1024 * 1024

NN = ((1,), (0,))
NT = ((1,), (1,))


def _cparams(sem):
    return pltpu.CompilerParams(dimension_semantics=sem, vmem_limit_bytes=VMEM_LIMIT)


def _dot(a, b, dims=NN):
    return lax.dot_general(a, b, (dims, ((), ())), preferred_element_type=F32)


def _split(x):
    hi = x.astype(BF16)
    lo = (x - hi.astype(F32)).astype(BF16)
    return hi, lo


def _mm3(a, b, dims=NN):
    return _dot(a[0], b[0], dims) + (_dot(a[0], b[1], dims) + _dot(a[1], b[0], dims))


def _layer_norm(x, g, b):
    mu = jnp.mean(x, axis=-1, keepdims=True)
    xc = x - mu
    var = jnp.mean(xc * xc, axis=-1, keepdims=True)
    return xc * lax.rsqrt(var + LN_EPS) * g + b


def _sigmoid(x):
    return 1.0 / (1.0 + jnp.exp(-x))


def _lane_head(shape, dim):
    return lax.broadcasted_iota(jnp.int32, shape, dim) // HEAD


def _block_diag(x):
    head = _lane_head(x.shape, 1)
    return jnp.concatenate([jnp.where(head == h, x, 0.0) for h in range(GROUP)], axis=0)


def _head_ones():
    return jnp.where(_lane_head((GW, GW), 0) == _lane_head((GW, GW), 1), 1.0, 0.0).astype(BF16)


def _head_sum(x, ones):
    hi, lo = _split(x)
    return _dot(hi, ones) + _dot(lo, ones)


def _mlp_kernel(x_ref, wu_ref, wd_ref, g_ref, b_ref, o_ref, acc_ref, xb_ref, *, alpha):
    j = pl.program_id(1)

    @pl.when(j == 0)
    def _():
        xb_ref[...] = x_ref[...].astype(BF16)
        acc_ref[...] = jnp.zeros_like(acc_ref)

    h = jnp.maximum(_dot(xb_ref[...], wu_ref[...]), 0.0)
    acc_ref[...] += _dot((h * h).astype(BF16), wd_ref[...])

    @pl.when(j == pl.num_programs(1) - 1)
    def _():
        o_ref[...] = _layer_norm(alpha * x_ref[...] + acc_ref[...], g_ref[...], b_ref[...])


def _mlp(x, w_up, w_down, g, b, alpha, tm, tf):
    rows, d = x.shape
    dff = w_up.shape[1]
    return pl.pallas_call(
        functools.partial(_mlp_kernel, alpha=alpha),
        grid=(rows // tm, dff // tf),
        in_specs=[
            pl.BlockSpec((tm, d), lambda i, j: (i, 0)),
            pl.BlockSpec((d, tf), lambda i, j: (0, j)),
            pl.BlockSpec((tf, d), lambda i, j: (j, 0)),
            pl.BlockSpec((1, d), lambda i, j: (0, 0)),
            pl.BlockSpec((1, d), lambda i, j: (0, 0)),
        ],
        out_specs=pl.BlockSpec((tm, d), lambda i, j: (i, 0)),
        out_shape=jax.ShapeDtypeStruct((rows, d), F32),
        scratch_shapes=[pltpu.VMEM((tm, d), F32), pltpu.VMEM((tm, d), BF16)],
        compiler_params=_cparams(("parallel", "arbitrary")),
        name="mlp",
    )(x, w_up, w_down, g, b)


def _rwkv_proj_kernel(x_ref, prev_ref, mu_ref, w1_ref, a1_ref, g1_ref,
                      wr_ref, wk_ref, wv_ref, w2_ref, a2_ref, g2_ref,
                      w0_ref, a0_ref, kk_ref, ka_ref, rk_ref,
                      r_o, lw_o, k_o, v_o, kap_o, b_o, g_o, bonus_o,
                      xm_ref, hw_ref, ha_ref, hg_ref):
    j = pl.program_id(2)

    @pl.when(j == 0)
    def _():
        x = x_ref[0]
        rolled = pltpu.roll(x, 1, axis=0)
        first = lax.broadcasted_iota(jnp.int32, x.shape, 0) == 0
        xx = jnp.where(first, prev_ref[0, 0], rolled) - x
        mu = mu_ref[...]
        xm_ref[0] = (x + xx * mu[0:1]).astype(BF16)
        xm_ref[1] = (x + xx * mu[2:3]).astype(BF16)
        xm_ref[2] = (x + xx * mu[3:4]).astype(BF16)
        xw = (x + xx * mu[1:2]).astype(BF16)
        xa = (x + xx * mu[4:5]).astype(BF16)
        xg = (x + xx * mu[5:6]).astype(BF16)
        hw_ref[...] = jnp.tanh(_dot(xw, w1_ref[...])).astype(BF16)
        ha_ref[...] = _dot(xa, a1_ref[...]).astype(BF16)
        hg_ref[...] = _sigmoid(_dot(xg, g1_ref[...])).astype(BF16)

    r = _dot(xm_ref[0], wr_ref[...])
    k = _dot(xm_ref[1], wk_ref[...])
    v = _dot(xm_ref[2], wv_ref[...])
    z = -(w0_ref[...] + _dot(hw_ref[...], w2_ref[...]))
    w_log = -(jnp.maximum(z, 0.0) + jnp.log(1.0 + jnp.exp(-jnp.abs(z)))) - 0.5
    a = _sigmoid(a0_ref[...] + _dot(ha_ref[...], a2_ref[...]))
    g = _dot(hg_ref[...], g2_ref[...])

    kk = k * kk_ref[...]
    ones = _head_ones()
    slabs = range(0, kk.shape[1], GW)
    sq = kk * kk
    ss = jnp.concatenate([_head_sum(sq[:, s:s + GW], ones) for s in slabs], axis=1)
    kap = kk * lax.rsqrt(jnp.maximum(ss, 1e-24))
    k2 = k * (1.0 + (a - 1.0) * ka_ref[...])
    rkk = r * k2 * rk_ref[...]
    bonus = jnp.concatenate([_head_sum(rkk[:, s:s + GW], ones) for s in slabs], axis=1) * v

    r_o[0] = r
    lw_o[0] = -jnp.exp(w_log)
    k_o[0] = k2
    v_o[0] = v
    kap_o[0] = kap
    b_o[0] = kap * a
    g_o[0] = g
    bonus_o[0] = bonus


def _rwkv_proj(x, prev, p, tm, tn):
    bsz, t, d = x.shape
    nt = t // tm
    lw, la, lg = p['w1'].shape[1], p['a1'].shape[1], p['g1'].shape[1]
    full = lambda shape: pl.BlockSpec(shape, lambda b, i, j: (0,) * len(shape))
    col = lambda rows: pl.BlockSpec((rows, tn), lambda b, i, j: (0, j))
    out_spec = pl.BlockSpec((1, tm, tn), lambda b, i, j: (b, i, j))
    out_sds = jax.ShapeDtypeStruct((bsz, t, d), F32)
    return pl.pallas_call(
        _rwkv_proj_kernel,
        grid=(bsz, nt, d // tn),
        in_specs=[
            pl.BlockSpec((1, tm, d), lambda b, i, j: (b, i, 0)),
            pl.BlockSpec((1, 1, 1, d), lambda b, i, j: (b, i, 0, 0)),
            full((8, d)), full((d, lw)), full((d, la)), full((d, lg)),
            col(d), col(d), col(d), col(lw), col(la), col(lg),
            col(1), col(1), col(1), col(1), col(1),
        ],
        out_specs=[out_spec] * 8,
        out_shape=[out_sds] * 8,
        scratch_shapes=[pltpu.VMEM((3, tm, d), BF16), pltpu.VMEM((tm, lw), BF16),
                        pltpu.VMEM((tm, la), BF16), pltpu.VMEM((tm, lg), BF16)],
        compiler_params=_cparams(("parallel", "parallel", "arbitrary")),
        name="rwkv_proj",
    )(x, prev, p['mu'], p['w1'], p['a1'], p['g1'], p['w_r'], p['w_k'], p['w_v'],
      p['w2'], p['a2'], p['g2'], p['w0'], p['a0'], p['k_k'], p['k_a'], p['r_k'])


def _head_transpose(x):
    xt = x.T
    return jnp.concatenate([xt[h * HEAD:(h + 1) * HEAD, :] for h in range(GROUP)], axis=1)


def _b16(x):
    return x.astype(BF16)


def _each(f, *lists):
    return [f(*args) for args in zip(*lists)]


def _bd16(x):
    return _b16(_block_diag(x))


def _wkv_chunks(r, lw, k, v, kap, b):
    c = CHUNK
    row = lax.broadcasted_iota(jnp.int32, (c, GW), 0)
    pos = lax.broadcasted_iota(jnp.int32, (c, GW), 1) % HEAD
    tri = jnp.where(lax.broadcasted_iota(jnp.int32, (c, c), 1) <= lax.broadcasted_iota(jnp.int32, (c, c), 0),
                    1.0, 0.0).astype(BF16)
    strict = pos < row
    lower = pos <= row
    diag = pos == row

    def cumsum(x):
        hi = _b16(x)
        rem = x - hi.astype(F32)
        mid = _b16(rem)
        lo = _b16(rem - mid.astype(F32))
        return _dot(tri, hi) + (_dot(tri, mid) + _dot(tri, lo))

    cum = _each(cumsum, lw)
    end = _each(lambda s: s[c - 1:c, :], cum)
    e_neg = _each(lambda s: jnp.exp(-s), cum)
    e_end = _each(lambda e, s: jnp.exp(e - s), end, cum)
    rt = _each(lambda x, s: x * jnp.exp(s), r, cum)
    kq = _each(lambda x, s, l: x * jnp.exp(s - l), kap, cum, lw)
    bh = _each(jnp.multiply, b, e_neg)
    kh = _each(jnp.multiply, k, e_neg)
    bp = _each(jnp.multiply, b, e_end)
    kp = _each(jnp.multiply, k, e_end)

    lhs = _each(lambda x, y: _b16(jnp.concatenate([x, y], axis=0)), kq, rt)
    gb = _each(lambda x, y: _dot(x, _bd16(y), NT), lhs, bh)
    gk = _each(lambda x, y: _dot(x, _bd16(y), NT), lhs, kh)
    l_ub = _each(lambda g: jnp.where(strict, g[:c], 0.0), gb)
    a_uk = _each(lambda g: jnp.where(strict, g[:c], 0.0), gk)
    a_rb = _each(lambda g: jnp.where(lower, g[c:], 0.0), gb)
    a_rk = _each(lambda g: jnp.where(lower, g[c:], 0.0), gk)

    stack = lambda x, y: _b16(jnp.concatenate([x, y], axis=0))
    nm = _each(jnp.negative, l_ub)
    pw = _each(lambda x: _dot(_b16(x), _bd16(x)), l_ub)
    for _ in range(int(math.log2(c)) - 2):
        res = _each(lambda x, n: _dot(stack(x, n), _bd16(x)), pw, nm)
        nm = _each(lambda n, x, y: n + x + y[c:], nm, pw, res)
        pw = _each(lambda y: y[:c], res)
    nm = _each(lambda n, x: n + x + _dot(_b16(n), _bd16(x)), nm, pw)
    nm16 = _each(_b16, nm)

    bpx = _each(lambda x: _b16(_head_transpose(x)), bp)
    kpx = _each(lambda x: _b16(_head_transpose(x)), kp)
    wa = _each(lambda x, y, z, w: _dot(jnp.concatenate([stack(x, y), z], axis=0), _bd16(w)), a_uk, a_rk, kpx, v)
    p = _each(lambda x, n: x + _dot(n, _bd16(x)), kq, nm16)
    u0 = _each(lambda x, n: -(x[:c] + _dot(n, _bd16(x[:c]))), wa, nm16)

    lhs2 = _each(lambda a, x: jnp.concatenate([_b16(a), x], axis=0), a_rb, bpx)
    rp = _each(lambda a, y: _dot(a, _bd16(y)), lhs2, p)
    ru = _each(lambda a, y: _dot(a, _bd16(y)), lhs2, u0)
    qp = _each(lambda x, y: x - y[:c], rt, rp)
    y0 = _each(lambda y, x: y[:c] + x[c:2 * c], ru, wa)
    m_cat = _each(lambda e, y: jnp.where(diag, jnp.exp(e), 0.0) - y[c:], end, rp)
    d_cat = _each(lambda y, x: y[c:] + x[2 * c:], ru, wa)
    return qp, y0, m_cat, d_cat


def _wkv_pre_kernel(r_ref, lw_ref, k_ref, v_ref, kap_ref, b_ref, qp_o, y0_o, m_o, d_o, *, nchunk):
    rows = [slice(ci * CHUNK, (ci + 1) * CHUNK) for ci in range(nchunk)]
    load = lambda ref: [ref[0, rs, :] for rs in rows]
    qp, y0, m_cat, d_cat = _wkv_chunks(load(r_ref), load(lw_ref), load(k_ref), load(v_ref),
                                       load(kap_ref), load(b_ref))
    for ci, rs in enumerate(rows):
        qp_o[0, rs, :] = qp[ci]
        y0_o[0, rs, :] = y0[ci]
        m_o[0, ci] = m_cat[ci]
        d_o[0, ci] = d_cat[ci]


def _wkv_pre(r, lw, k, v, kap, b, tc):
    bsz, t, d = r.shape
    nc = t // CHUNK
    seq = pl.BlockSpec((1, tc * CHUNK, GW), lambda bi, ci, gi: (bi, ci, gi))
    mat = pl.BlockSpec((1, tc, HEAD, GW), lambda bi, ci, gi: (bi, ci, 0, gi))
    seq_sds = jax.ShapeDtypeStruct((bsz, t, d), F32)
    mat_sds = jax.ShapeDtypeStruct((bsz, nc, HEAD, d), F32)
    return pl.pallas_call(
        functools.partial(_wkv_pre_kernel, nchunk=tc),
        grid=(bsz, nc // tc, d // GW),
        in_specs=[seq] * 6,
        out_specs=[seq, seq, mat, mat],
        out_shape=[seq_sds, seq_sds, mat_sds, mat_sds],
        compiler_params=_cparams(("parallel", "parallel", "parallel")),
        name="wkv_pre",
    )(r, lw, k, v, kap, b)


def _wkv_seq_kernel(qp_ref, y0_ref, m_ref, d_ref, h0_ref, y_o, h_o, h_ref, *, nb, nchunk, ngroup):
    ci = pl.program_id(1)

    @pl.when(ci == 0)
    def _():
        h_ref[...] = h0_ref[...]

    def body(c, carry):
        rows = pl.ds(pl.multiple_of(c * CHUNK, CHUNK), CHUNK)
        for bi in range(nb):
            for gi in range(ngroup):
                lanes = slice(gi * GW, (gi + 1) * GW)
                hbd = _split(_block_diag(h_ref[bi, :, lanes]))
                lhs = _split(jnp.concatenate([qp_ref[bi, rows, lanes], m_ref[bi, c, :, lanes]], axis=0))
                res = _mm3(lhs, hbd)
                y_o[bi, rows, lanes] = res[:CHUNK] + y0_ref[bi, rows, lanes]
                h_ref[bi, :, lanes] = res[CHUNK:] + d_ref[bi, c, :, lanes]
        return carry

    lax.fori_loop(0, nchunk, body, 0)

    @pl.when(ci == pl.num_programs(1) - 1)
    def _():
        h_o[...] = h_ref[...]


def _wkv_seq(qp, y0, m, dmat, h0, nb, tc):
    bsz, t, d = qp.shape
    nc = t // CHUNK
    seq = pl.BlockSpec((nb, tc * CHUNK, d), lambda bi, ci: (bi, ci, 0))
    mat = pl.BlockSpec((nb, tc, HEAD, d), lambda bi, ci: (bi, ci, 0, 0))
    st = pl.BlockSpec((nb, HEAD, d), lambda bi, ci: (bi, 0, 0))
    return pl.pallas_call(
        functools.partial(_wkv_seq_kernel, nb=nb, nchunk=tc, ngroup=d // GW),
        grid=(bsz // nb, nc // tc),
        in_specs=[seq, seq, mat, mat, st],
        out_specs=[seq, st],
        out_shape=[jax.ShapeDtypeStruct((bsz, t, d), F32), jax.ShapeDtypeStruct((bsz, HEAD, d), F32)],
        scratch_shapes=[pltpu.VMEM((nb, HEAD, d), F32)],
        compiler_params=_cparams(("parallel", "arbitrary")),
        name="wkv_seq",
    )(qp, y0, m, dmat, h0)


def _rwkv_out_kernel(y_ref, bonus_ref, g_ref, x_ref, wo_ref, gng_ref, gnb_ref, lng_ref, lnb_ref, o_ref, *, alpha):
    ones = _head_ones()
    y = y_ref[...]
    parts = []
    for s in range(0, y.shape[1], GW):
        ys = y[:, s:s + GW]
        mean = _head_sum(ys, ones) * (1.0 / HEAD)
        yc = ys - mean
        var = _head_sum(yc * yc, ones) * (1.0 / HEAD)
        parts.append(yc * lax.rsqrt(var + GN_EPS))
    yn = jnp.concatenate(parts, axis=1) * gng_ref[...] + gnb_ref[...]
    z = ((yn + bonus_ref[...]) * g_ref[...]).astype(BF16)
    out = _dot(z, wo_ref[...])
    o_ref[...] = _layer_norm(alpha * x_ref[...] + out, lng_ref[...], lnb_ref[...])


def _rwkv_out(y, bonus, g, x, w_o, gn_g, gn_b, ln_g, ln_b, alpha, tm):
    rows, d = x.shape
    tile = pl.BlockSpec((tm, d), lambda i: (i, 0))
    vec = pl.BlockSpec((1, d), lambda i: (0, 0))
    return pl.pallas_call(
        functools.partial(_rwkv_out_kernel, alpha=alpha),
        grid=(rows // tm,),
        in_specs=[tile, tile, tile, tile, pl.BlockSpec((d, d), lambda i: (0, 0)), vec, vec, vec, vec],
        out_specs=tile,
        out_shape=jax.ShapeDtypeStruct((rows, d), F32),
        compiler_params=_cparams(("parallel",)),
        name="rwkv_out",
    )(y, bonus, g, x, w_o, gn_g, gn_b, ln_g, ln_b)


def _glu_kernel(x_ref, wa_ref, wg_ref, ba_ref, bg_ref, u_ref, xb_ref):
    @pl.when(pl.program_id(1) == 0)
    def _():
        xb_ref[...] = x_ref[...].astype(BF16)

    xb = xb_ref[...]
    val = _dot(xb, wa_ref[...]) + ba_ref[...]
    gate = _dot(xb, wg_ref[...]) + bg_ref[...]
    u_ref[...] = val * _sigmoid(gate)


def _glu(x, w_in, b_in, tm, tn):
    rows, d = x.shape
    nj = d // tn
    return pl.pallas_call(
        _glu_kernel,
        grid=(rows // tm, nj),
        in_specs=[
            pl.BlockSpec((tm, d), lambda i, j: (i, 0)),
            pl.BlockSpec((d, tn), lambda i, j: (0, j)),
            pl.BlockSpec((d, tn), lambda i, j: (0, nj + j)),
            pl.BlockSpec((1, tn), lambda i, j: (0, j)),
            pl.BlockSpec((1, tn), lambda i, j: (0, nj + j)),
        ],
        out_specs=pl.BlockSpec((tm, tn), lambda i, j: (i, j)),
        out_shape=jax.ShapeDtypeStruct((rows, d), F32),
        scratch_shapes=[pltpu.VMEM((tm, d), BF16)],
        compiler_params=_cparams(("parallel", "arbitrary")),
        name="conv_glu",
    )(x, w_in, w_in, b_in, b_in)


def _conv_out_kernel(u_ref, halo_ref, st_ref, x_ref, dw_ref, dwb_ref, cg_ref, cb_ref, wo_ref, bo_ref,
                     lng_ref, lnb_ref, o_ref, win_ref, c_ref, *, alpha, tm):
    i = pl.program_id(1)
    d = u_ref.shape[2]

    @pl.when(i == 0)
    def _():
        win_ref[0:HALO, :] = st_ref[0]

    @pl.when(i > 0)
    def _():
        win_ref[0:HALO, :] = halo_ref[0]

    win_ref[HALO:HALO + tm, :] = u_ref[0]

    zrows = tm + 8
    for ls in range(0, d, GW):
        lanes = slice(ls, ls + GW)
        acc = jnp.zeros((tm, GW), F32)
        for s in range(8):
            z = None
            for q in range((HALO + 8) // 8):
                j = 8 * q + s - 2
                if j < 0 or j >= CONV_W:
                    continue
                nrows = tm if s == 0 else zrows
                term = win_ref[8 * q:8 * q + nrows, lanes] * dw_ref[j:j + 1, lanes]
                z = term if z is None else z + term
            if s == 0:
                acc = acc + z
            else:
                acc = acc + pltpu.roll(z, zrows - s, axis=0)[:tm]
        c_ref[:, lanes] = acc

    c = _layer_norm(c_ref[...] + dwb_ref[...], cg_ref[...], cb_ref[...])
    c = (c * _sigmoid(c)).astype(BF16)
    out = _dot(c, wo_ref[...]) + bo_ref[...]
    o_ref[0] = _layer_norm(alpha * x_ref[0] + out, lng_ref[...], lnb_ref[...])


def _conv_out(u, state_pad, x, dw, dw_b, cln_g, cln_b, w_out, b_out, ln_g, ln_b, alpha, tm):
    bsz, t, d = u.shape
    hb = tm // HALO
    tile = pl.BlockSpec((1, tm, d), lambda b, i: (b, i, 0))
    vec = pl.BlockSpec((1, d), lambda b, i: (0, 0))
    return pl.pallas_call(
        functools.partial(_conv_out_kernel, alpha=alpha, tm=tm),
        grid=(bsz, t // tm),
        in_specs=[
            tile,
            pl.BlockSpec((1, HALO, d), lambda b, i: (b, jnp.maximum(i * hb - 1, 0), 0)),
            pl.BlockSpec((1, HALO, d), lambda b, i: (b, 0, 0)),
            tile,
            pl.BlockSpec((HALO, d), lambda b, i: (0, 0)),
            vec, vec, vec,
            pl.BlockSpec((d, d), lambda b, i: (0, 0)),
            vec, vec, vec,
        ],
        out_specs=tile,
        out_shape=jax.ShapeDtypeStruct((bsz, t, d), F32),
        scratch_shapes=[pltpu.VMEM((HALO + tm, d), F32), pltpu.VMEM((tm, d), F32)],
        compiler_params=_cparams(("parallel", "arbitrary")),
        name="conv_out",
    )(u, u, state_pad, x, dw, dw_b, cln_g, cln_b, w_out, b_out, ln_g, ln_b)


def _pad_cols(w, n):
    return jnp.pad(w, ((0, 0), (0, n - w.shape[1])))


def _pad_rows(w, n):
    return jnp.pad(w, ((0, n - w.shape[0]), (0, 0)))


def _round_up(n, m):
    return -(-n // m) * m


def _prep_params(raw):
    depth = raw['w_up'].shape[0]
    n_rwkv = raw['w_r'].shape[0]
    n_conv = raw['c_w_in'].shape[0]
    row = lambda v: v.reshape(1, -1)
    rwkv = []
    for j in range(n_rwkv):
        lw = _round_up(raw['w1'].shape[2], 128)
        la = _round_up(raw['a1'].shape[2], 128)
        rwkv.append(dict(
            mu=_pad_rows(raw['mu'][j], 8),
            w_r=raw['w_r'][j].astype(BF16), w_k=raw['w_k'][j].astype(BF16), w_v=raw['w_v'][j].astype(BF16),
            w_o=raw['w_o'][j].astype(BF16),
            w1=_pad_cols(raw['w1'][j], lw).astype(BF16), w2=_pad_rows(raw['w2'][j], lw).astype(BF16),
            a1=_pad_cols(raw['a1'][j], la).astype(BF16), a2=_pad_rows(raw['a2'][j], la).astype(BF16),
            g1=raw['g1'][j].astype(BF16), g2=raw['g2'][j].astype(BF16),
            w0=row(raw['w0'][j]), a0=row(raw['a0'][j]), k_k=row(raw['k_k'][j]), k_a=row(raw['k_a'][j]),
            r_k=row(raw['r_k'][j]), gn_g=row(raw['gn_g'][j]), gn_b=row(raw['gn_b'][j])))
    conv = []
    for j in range(n_conv):
        conv.append(dict(
            w_in=raw['c_w_in'][j].astype(BF16), b_in=row(raw['c_b_in'][j]),
            dw=_pad_rows(raw['c_dw'][j], HALO), dw_b=row(raw['c_dw_b'][j]),
            ln_g=row(raw['c_ln_g'][j]), ln_b=row(raw['c_ln_b'][j]),
            w_out=raw['c_w_out'][j].astype(BF16), b_out=row(raw['c_b_out'][j])))
    mlp = [dict(w_up=raw['w_up'][i].astype(BF16), w_down=raw['w_down'][i].astype(BF16),
                mix_g=row(raw['ln_mix_g'][i]), mix_b=row(raw['ln_mix_b'][i]),
                ffn_g=row(raw['ln_ffn_g'][i]), ffn_b=row(raw['ln_ffn_b'][i])) for i in range(depth)]
    return rwkv, conv, mlp


def _rwkv_layer(x, shift, wkv, p, lnp, alpha, cfg):
    bsz, t, d = x.shape
    tm = cfg['tm_seq']
    prev = jnp.concatenate([shift[:, None, :], x[:, tm - 1:t - 1:tm, :]], axis=1)[:, :, None, :]
    r, lw, k, v, kap, b, g, bonus = _rwkv_proj(x, prev, p, tm, cfg['tn'])
    tp = _round_up(t, CHUNK)
    if tp != t:
        r, lw, k, v, kap, b = [jnp.pad(a, ((0, 0), (0, tp - t), (0, 0))) for a in (r, lw, k, v, kap, b)]
    qp, y0, m, dmat = _wkv_pre(r, lw, k, v, kap, b, cfg['tc_pre'])
    nh = d // HEAD
    h0 = jnp.transpose(wkv, (0, 3, 1, 2)).reshape(bsz, HEAD, d)
    y, h1 = _wkv_seq(qp, y0, m, dmat, h0, cfg['nb_seq'], cfg['tc_seq'])
    new_wkv = jnp.transpose(h1.reshape(bsz, HEAD, nh, HEAD), (0, 2, 3, 1))
    rows = bsz * t
    x1 = _rwkv_out(y[:, :t].reshape(rows, d), bonus.reshape(rows, d), g.reshape(rows, d),
                   x.reshape(rows, d), p['w_o'], p['gn_g'], p['gn_b'], lnp['mix_g'], lnp['mix_b'],
                   alpha, cfg['tm_out'])
    return x1.reshape(bsz, t, d), x[:, -1], new_wkv


def _conv_layer(x, state, p, lnp, alpha, cfg):
    bsz, t, d = x.shape
    u = _glu(x.reshape(bsz * t, d), p['w_in'], p['b_in'], cfg['tm_rows'], cfg['tn']).reshape(bsz, t, d)
    state_pad = jnp.pad(state, ((0, 0), (HALO - CONV_STATE, 0), (0, 0)))
    x1 = _conv_out(u, state_pad, x, p['dw'], p['dw_b'], p['ln_g'], p['ln_b'], p['w_out'], p['b_out'],
                   lnp['mix_g'], lnp['mix_b'], alpha, cfg['tm_conv'])
    keep = min(t, CONV_STATE)
    new_state = jnp.concatenate([state[:, keep:], u[:, t - keep:]], axis=1)
    return x1, new_state


def _run_trunk(x, st_shift, st_wkv, st_conv, params, cfg):
    rwkv, conv, mlp = params
    depth = len(mlp)
    alpha = (2.0 * depth) ** 0.25
    bsz, t, d = x.shape
    new_shift, new_wkv, new_conv = [], [], []
    for i in range(depth):
        j = i // 2
        if i % 2 == 0:
            x, sh, s = _rwkv_layer(x, st_shift[j], st_wkv[j], rwkv[j], mlp[i], alpha, cfg)
            new_shift.append(sh)
            new_wkv.append(s)
        else:
            x, cs = _conv_layer(x, st_conv[j], conv[j], mlp[i], alpha, cfg)
            new_conv.append(cs)
        x = _mlp(x.reshape(bsz * t, d), mlp[i]['w_up'], mlp[i]['w_down'], mlp[i]['ffn_g'], mlp[i]['ffn_b'],
                 alpha, cfg['tm_rows'], cfg['tf']).reshape(bsz, t, d)
    return x, jnp.stack(new_wkv), jnp.stack(new_shift), jnp.stack(new_conv)


def _config(bsz, t):
    rows = bsz * t
    tm_seq = min(t, 512)
    nchunk = _round_up(t, CHUNK) // CHUNK
    return dict(
        tm_seq=tm_seq, tn=512, tm_rows=min(rows, 512), tf=1024, tm_out=min(rows, 256),
        tm_conv=min(t, 256), tc_pre=min(nchunk, 8), tc_seq=min(nchunk, 2), nb_seq=min(bsz, 2))


def kernel(x_prompt, x_sample, state_wkv, state_shift, state_conv, mu, w_r, w_k, w_v, w_o, w0, w1, w2, a0, a1, a2, g1, g2, k_k, k_a, r_k, gn_g, gn_b, c_w_in, c_b_in, c_dw, c_dw_b, c_ln_g, c_ln_b, c_w_out, c_b_out, w_up, w_down, ln_mix_g, ln_mix_b, ln_ffn_g, ln_ffn_b):
    raw = dict(mu=mu, w_r=w_r, w_k=w_k, w_v=w_v, w_o=w_o, w0=w0, w1=w1, w2=w2, a0=a0, a1=a1, a2=a2,
               g1=g1, g2=g2, k_k=k_k, k_a=k_a, r_k=r_k, gn_g=gn_g, gn_b=gn_b, c_w_in=c_w_in,
               c_b_in=c_b_in, c_dw=c_dw, c_dw_b=c_dw_b, c_ln_g=c_ln_g, c_ln_b=c_ln_b, c_w_out=c_w_out,
               c_b_out=c_b_out, w_up=w_up, w_down=w_down, ln_mix_g=ln_mix_g, ln_mix_b=ln_mix_b,
               ln_ffn_g=ln_ffn_g, ln_ffn_b=ln_ffn_b)
    params = _prep_params(raw)
    bp, tp, d = x_prompt.shape
    bs, ts, _ = x_sample.shape
    n_rwkv, n_conv = w_r.shape[0], c_w_in.shape[0]
    nh = d // HEAD
    z_wkv = jnp.zeros((n_rwkv, bp, nh, HEAD, HEAD), F32)
    z_shift = jnp.zeros((n_rwkv, bp, d), F32)
    z_conv = jnp.zeros((n_conv, bp, CONV_STATE, d), F32)
    y_p, wkv_p, shift_p, conv_p = _run_trunk(x_prompt, z_shift, z_wkv, z_conv, params, _config(bp, tp))
    y_s, wkv_s, shift_s, conv_s = _run_trunk(x_sample, state_shift, state_wkv, state_conv, params, _config(bs, ts))
    return (y_p, y_s, wkv_p, shift_p, conv_p, wkv_s, shift_s, conv_s)
```

```python
import functools
import math

import jax
import jax.numpy as jnp
from jax import lax
from jax.experimental import pallas as pl
from jax.experimental.pallas import tpu as pltpu

F32 = jnp.float32
BF16 = jnp.bfloat16

SUBLANE = 8
HEAD = 64
GROUP = 4
GW = HEAD * GROUP
CHUNK = 64
WKV_UNITS = 8
CONV_W = 31
CONV_STATE = CONV_W - 1
HALO = 32
LN_EPS = 1e-5
GN_EPS = 64e-5
VMEM_LIMIT = 56 * 1024 * 1024

NN = ((1,), (0,))
NT = ((1,), (1,))


def _cparams(sem):
    return pltpu.CompilerParams(dimension_semantics=sem, vmem_limit_bytes=VMEM_LIMIT)


def _dot(a, b, dims=NN):
    return lax.dot_general(a, b, (dims, ((), ())), preferred_element_type=F32)


def _split(x):
    hi = x.astype(BF16)
    lo = (x - hi.astype(F32)).astype(BF16)
    return hi, lo


def _layer_norm(x, g, b):
    mu = jnp.mean(x, axis=-1, keepdims=True)
    xc = x - mu
    var = jnp.mean(xc * xc, axis=-1, keepdims=True)
    return xc * lax.rsqrt(var + LN_EPS) * g + b


def _sigmoid(x):
    return 1.0 / (1.0 + jnp.exp(-x))


def _lane_head(shape, dim):
    return lax.broadcasted_iota(jnp.int32, shape, dim) // HEAD


def _block_diag(x):
    head = _lane_head(x.shape, 1)
    return jnp.concatenate([jnp.where(head == h, x, 0.0) for h in range(GROUP)], axis=0)


def _head_ones():
    return jnp.where(_lane_head((GW, GW), 0) == _lane_head((GW, GW), 1), 1.0, 0.0).astype(BF16)


def _head_sum(x, ones):
    hi, lo = _split(x)
    return _dot(hi, ones) + _dot(lo, ones)


def _mlp_kernel(x_ref, wu_ref, wd_ref, g_ref, b_ref, o_ref, acc_ref, xb_ref, *, alpha):
    j = pl.program_id(1)

    @pl.when(j == 0)
    def _():
        xb_ref[...] = x_ref[...].astype(BF16)
        acc_ref[...] = jnp.zeros_like(acc_ref)

    h = jnp.maximum(_dot(xb_ref[...], wu_ref[...]), 0.0)
    acc_ref[...] += _dot((h * h).astype(BF16), wd_ref[...])

    @pl.when(j == pl.num_programs(1) - 1)
    def _():
        o_ref[...] = _layer_norm(alpha * x_ref[...] + acc_ref[...], g_ref[...], b_ref[...])


def _mlp(x, w_up, w_down, g, b, alpha, tm, tf):
    rows, d = x.shape
    dff = w_up.shape[1]
    return pl.pallas_call(
        functools.partial(_mlp_kernel, alpha=alpha),
        grid=(rows // tm, dff // tf),
        in_specs=[
            pl.BlockSpec((tm, d), lambda i, j: (i, 0)),
            pl.BlockSpec((d, tf), lambda i, j: (0, j)),
            pl.BlockSpec((tf, d), lambda i, j: (j, 0)),
            pl.BlockSpec((1, d), lambda i, j: (0, 0)),
            pl.BlockSpec((1, d), lambda i, j: (0, 0)),
        ],
        out_specs=pl.BlockSpec((tm, d), lambda i, j: (i, 0)),
        out_shape=jax.ShapeDtypeStruct((rows, d), F32),
        scratch_shapes=[pltpu.VMEM((tm, d), F32), pltpu.VMEM((tm, d), BF16)],
        compiler_params=_cparams(("parallel", "arbitrary")),
        name="mlp",
    )(x, w_up, w_down, g, b)


def _rwkv_proj_kernel(x_ref, tail_ref, shift_ref, mu_ref, w1_ref, a1_ref, g1_ref,
                      wr_ref, wk_ref, wv_ref, w2_ref, a2_ref, g2_ref,
                      w0_ref, a0_ref, kk_ref, ka_ref, rk_ref,
                      r_o, lw_o, k_o, v_o, kap_o, b_o, g_o, bonus_o,
                      xm_ref, hw_ref, ha_ref, hg_ref):
    j = pl.program_id(2)

    @pl.when(j == 0)
    def _():
        x = x_ref[0]
        rolled = pltpu.roll(x, 1, axis=0)
        first = lax.broadcasted_iota(jnp.int32, x.shape, 0) == 0
        before = jnp.where(pl.program_id(1) == 0, shift_ref[0], tail_ref[0, SUBLANE - 1:SUBLANE, :])
        xx = jnp.where(first, before, rolled) - x
        mu = mu_ref[...]
        xm_ref[0] = (x + xx * mu[0:1]).astype(BF16)
        xm_ref[1] = (x + xx * mu[2:3]).astype(BF16)
        xm_ref[2] = (x + xx * mu[3:4]).astype(BF16)
        xw = (x + xx * mu[1:2]).astype(BF16)
        xa = (x + xx * mu[4:5]).astype(BF16)
        xg = (x + xx * mu[5:6]).astype(BF16)
        hw_ref[...] = jnp.tanh(_dot(xw, w1_ref[...])).astype(BF16)
        ha_ref[...] = _dot(xa, a1_ref[...]).astype(BF16)
        hg_ref[...] = _sigmoid(_dot(xg, g1_ref[...])).astype(BF16)

    r = _dot(xm_ref[0], wr_ref[...])
    k = _dot(xm_ref[1], wk_ref[...])
    v = _dot(xm_ref[2], wv_ref[...])
    z = -(w0_ref[...] + _dot(hw_ref[...], w2_ref[...]))
    w_log = -(jnp.maximum(z, 0.0) + jnp.log(1.0 + jnp.exp(-jnp.abs(z)))) - 0.5
    a = _sigmoid(a0_ref[...] + _dot(ha_ref[...], a2_ref[...]))
    g = _dot(hg_ref[...], g2_ref[...])

    kk = k * kk_ref[...]
    ones = _head_ones()
    slabs = range(0, kk.shape[1], GW)
    sq = kk * kk
    ss = jnp.concatenate([_head_sum(sq[:, s:s + GW], ones) for s in slabs], axis=1)
    kap = kk * lax.rsqrt(jnp.maximum(ss, 1e-24))
    k2 = k * (1.0 + (a - 1.0) * ka_ref[...])
    rkk = r * k2 * rk_ref[...]
    bonus = jnp.concatenate([_head_sum(rkk[:, s:s + GW], ones) for s in slabs], axis=1) * v

    r_o[0] = r
    lw_o[0] = -jnp.exp(w_log)
    k_o[0] = k2
    v_o[0] = v
    kap_o[0] = kap
    b_o[0] = kap * a
    g_o[0] = g
    bonus_o[0] = bonus


def _rwkv_proj(x, shift, p, tm, tn):
    bsz, t, d = x.shape
    nt = t // tm
    tail_blocks = tm // SUBLANE
    lw, la, lg = p['w1'].shape[1], p['a1'].shape[1], p['g1'].shape[1]
    full = lambda shape: pl.BlockSpec(shape, lambda b, i, j: (0,) * len(shape))
    col = lambda rows: pl.BlockSpec((rows, tn), lambda b, i, j: (0, j))
    out_spec = pl.BlockSpec((1, tm, tn), lambda b, i, j: (b, i, j))
    out_sds = jax.ShapeDtypeStruct((bsz, t, d), F32)
    return pl.pallas_call(
        _rwkv_proj_kernel,
        grid=(bsz, nt, d // tn),
        in_specs=[
            pl.BlockSpec((1, tm, d), lambda b, i, j: (b, i, 0)),
            pl.BlockSpec((1, SUBLANE, d), lambda b, i, j: (b, jnp.maximum(i * tail_blocks - 1, 0), 0)),
            pl.BlockSpec((1, 1, d), lambda b, i, j: (b, 0, 0)),
            full((8, d)), full((d, lw)), full((d, la)), full((d, lg)),
            col(d), col(d), col(d), col(lw), col(la), col(lg),
            col(1), col(1), col(1), col(1), col(1),
        ],
        out_specs=[out_spec] * 8,
        out_shape=[out_sds] * 8,
        scratch_shapes=[pltpu.VMEM((3, tm, d), BF16), pltpu.VMEM((tm, lw), BF16),
                        pltpu.VMEM((tm, la), BF16), pltpu.VMEM((tm, lg), BF16)],
        compiler_params=_cparams(("parallel", "parallel", "arbitrary")),
        name="rwkv_proj",
    )(x, x, shift[:, None, :], p['mu'], p['w1'], p['a1'], p['g1'], p['w_r'], p['w_k'], p['w_v'],
      p['w2'], p['a2'], p['g2'], p['w0'], p['a0'], p['k_k'], p['k_a'], p['r_k'])


def _head_transpose(x):
    xt = x.T
    return jnp.concatenate([xt[h * HEAD:(h + 1) * HEAD, :] for h in range(GROUP)], axis=1)


def _b16(x):
    return x.astype(BF16)


def _each(f, *lists):
    return [f(*args) for args in zip(*lists)]


def _bd16(x):
    return _b16(_block_diag(x))


def _wkv_chunks(r, lw, k, v, kap, b):
    c = CHUNK
    row = lax.broadcasted_iota(jnp.int32, (c, GW), 0)
    pos = lax.broadcasted_iota(jnp.int32, (c, GW), 1) % HEAD
    tri = jnp.where(lax.broadcasted_iota(jnp.int32, (c, c), 1) <= lax.broadcasted_iota(jnp.int32, (c, c), 0),
                    1.0, 0.0).astype(BF16)
    strict = pos < row
    lower = pos <= row
    diag = pos == row

    def cumsum(x):
        hi = _b16(x)
        rem = x - hi.astype(F32)
        mid = _b16(rem)
        lo = _b16(rem - mid.astype(F32))
        return _dot(tri, hi) + (_dot(tri, mid) + _dot(tri, lo))

    cum = _each(cumsum, lw)
    end = _each(lambda s: s[c - 1:c, :], cum)
    e_neg = _each(lambda s: jnp.exp(-s), cum)
    e_end = _each(lambda e, s: jnp.exp(e - s), end, cum)
    rt = _each(lambda x, s: x * jnp.exp(s), r, cum)
    kq = _each(lambda x, s, l: x * jnp.exp(s - l), kap, cum, lw)
    bh = _each(jnp.multiply, b, e_neg)
    kh = _each(jnp.multiply, k, e_neg)
    bp = _each(jnp.multiply, b, e_end)
    kp = _each(jnp.multiply, k, e_end)

    lhs = _each(lambda x, y: _b16(jnp.concatenate([x, y], axis=0)), kq, rt)
    gb = _each(lambda x, y: _dot(x, _bd16(y), NT), lhs, bh)
    gk = _each(lambda x, y: _dot(x, _bd16(y), NT), lhs, kh)
    l_ub = _each(lambda g: jnp.where(strict, g[:c], 0.0), gb)
    a_uk = _each(lambda g: jnp.where(strict, g[:c], 0.0), gk)
    a_rb = _each(lambda g: jnp.where(lower, g[c:], 0.0), gb)
    a_rk = _each(lambda g: jnp.where(lower, g[c:], 0.0), gk)

    stack = lambda x, y: _b16(jnp.concatenate([x, y], axis=0))
    nm = _each(jnp.negative, l_ub)
    pw = _each(lambda x: _dot(_b16(x), _bd16(x)), l_ub)
    for _ in range(int(math.log2(c)) - 2):
        res = _each(lambda x, n: _dot(stack(x, n), _bd16(x)), pw, nm)
        nm = _each(lambda n, x, y: n + x + y[c:], nm, pw, res)
        pw = _each(lambda y: y[:c], res)
    nm = _each(lambda n, x: n + x + _dot(_b16(n), _bd16(x)), nm, pw)
    nm16 = _each(_b16, nm)

    bpx = _each(lambda x: _b16(_head_transpose(x)), bp)
    kpx = _each(lambda x: _b16(_head_transpose(x)), kp)
    wa = _each(lambda x, y, z, w: _dot(jnp.concatenate([stack(x, y), z], axis=0), _bd16(w)), a_uk, a_rk, kpx, v)
    p = _each(lambda x, n: x + _dot(n, _bd16(x)), kq, nm16)
    u0 = _each(lambda x, n: -(x[:c] + _dot(n, _bd16(x[:c]))), wa, nm16)

    lhs2 = _each(lambda a, x: jnp.concatenate([_b16(a), x], axis=0), a_rb, bpx)
    rp = _each(lambda a, y: _dot(a, _bd16(y)), lhs2, p)
    ru = _each(lambda a, y: _dot(a, _bd16(y)), lhs2, u0)
    qp = _each(lambda x, y: x - y[:c], rt, rp)
    y0 = _each(lambda y, x: y[:c] + x[c:2 * c], ru, wa)
    m_cat = _each(lambda e, y: jnp.where(diag, jnp.exp(e), 0.0) - y[c:], end, rp)
    d_cat = _each(lambda y, x: y[c:] + x[2 * c:], ru, wa)
    return qp, y0, m_cat, d_cat


def _wkv_pre_kernel(r_ref, lw_ref, k_ref, v_ref, kap_ref, b_ref, qp_o, y0_o, m_o, d_o, *, nchunk, ngroup):
    units = [(ci, slice(ci * CHUNK, (ci + 1) * CHUNK), slice(gi * GW, (gi + 1) * GW))
             for ci in range(nchunk) for gi in range(ngroup)]
    load = lambda ref: [ref[0, rs, ls] for _, rs, ls in units]
    qp, y0, m_cat, d_cat = _wkv_chunks(load(r_ref), load(lw_ref), load(k_ref), load(v_ref),
                                       load(kap_ref), load(b_ref))
    for n, (ci, rs, ls) in enumerate(units):
        qp_o[0, rs, ls] = qp[n]
        y0_o[0, rs, ls] = y0[n]
        m_o[0, ci, :, ls] = m_cat[n]
        d_o[0, ci, :, ls] = d_cat[n]


def _wkv_pre(r, lw, k, v, kap, b, tc, tg):
    bsz, t, d = r.shape
    nc = t // CHUNK
    seq = pl.BlockSpec((1, tc * CHUNK, tg * GW), lambda bi, ci, gi: (bi, ci, gi))
    mat = pl.BlockSpec((1, tc, HEAD, tg * GW), lambda bi, ci, gi: (bi, ci, 0, gi))
    seq_sds = jax.ShapeDtypeStruct((bsz, t, d), F32)
    mat_sds = jax.ShapeDtypeStruct((bsz, nc, HEAD, d), F32)
    return pl.pallas_call(
        functools.partial(_wkv_pre_kernel, nchunk=tc, ngroup=tg),
        grid=(bsz, nc // tc, d // (tg * GW)),
        in_specs=[seq] * 6,
        out_specs=[seq, seq, mat, mat],
        out_shape=[seq_sds, seq_sds, mat_sds, mat_sds],
        compiler_params=_cparams(("parallel", "parallel", "parallel")),
        name="wkv_pre",
    )(r, lw, k, v, kap, b)


def _wkv_tail_kernel(qp_ref, y0_ref, m_ref, d_ref, h0_ref, bonus_ref, g_ref, x_ref, wo_ref,
                     gng_ref, gnb_ref, lng_ref, lnb_ref, o_ref, h_o, h_ref, y_ref,
                     *, nb, nchunk, ngroup, alpha):
    s = pl.program_id(1)
    rows_b = nchunk * CHUNK

    @pl.when(s == 0)
    def _():
        h_ref[...] = h0_ref[...]
        y_ref[...] = jnp.zeros_like(y_ref)

    ones = _head_ones()
    y = y_ref[...]
    parts = []
    for ls in range(0, y.shape[1], GW):
        ys = y[:, ls:ls + GW]
        mean = _dot(_b16(ys), ones) * (1.0 / HEAD)
        yc = ys - mean
        var = _dot(_b16(yc * yc), ones) * (1.0 / HEAD)
        parts.append(yc * lax.rsqrt(var + GN_EPS))
    yn = jnp.concatenate(parts, axis=1) * gng_ref[...] + gnb_ref[...]
    rows_of = lambda ref: jnp.concatenate([ref[bi] for bi in range(nb)], axis=0)
    z = _b16((yn + rows_of(bonus_ref)) * rows_of(g_ref))
    res = _layer_norm(alpha * rows_of(x_ref) + _dot(z, wo_ref[...]), lng_ref[...], lnb_ref[...])
    for bi in range(nb):
        o_ref[bi] = res[bi * rows_b:(bi + 1) * rows_b]

    for c in range(nchunk):
        rows = slice(c * CHUNK, (c + 1) * CHUNK)
        for bi in range(nb):
            for gi in range(ngroup):
                lanes = slice(gi * GW, (gi + 1) * GW)
                h_hi, h_lo = _split(_block_diag(h_ref[bi, :, lanes]))
                m_hi, m_lo = _split(m_ref[bi, c, :, lanes])
                lhs = jnp.concatenate([_b16(qp_ref[bi, rows, lanes]), m_hi, m_lo], axis=0)
                out = _dot(lhs, h_hi)
                y_ref[bi * rows_b + c * CHUNK:bi * rows_b + (c + 1) * CHUNK, lanes] = (
                    out[:CHUNK] + y0_ref[bi, rows, lanes])
                h_ref[bi, :, lanes] = (out[CHUNK:2 * CHUNK] + (out[2 * CHUNK:] + _dot(m_hi, h_lo))
                                       + d_ref[bi, c, :, lanes])

    @pl.when(s == pl.num_programs(1) - 2)
    def _():
        h_o[...] = h_ref[...]


def _wkv_tail(qp, y0, m, dmat, h0, bonus, g, x, w_o, gn_g, gn_b, ln_g, ln_b, alpha, nb, tc):
    bsz, t, d = qp.shape
    steps = t // (tc * CHUNK)
    cur = lambda bi, s: (bi, jnp.minimum(s, steps - 1), 0)
    prv = lambda bi, s: (bi, jnp.maximum(s - 1, 0), 0)
    seq = pl.BlockSpec((nb, tc * CHUNK, d), cur)
    mat = pl.BlockSpec((nb, tc, HEAD, d), lambda bi, s: (bi, jnp.minimum(s, steps - 1), 0, 0))
    st = pl.BlockSpec((nb, HEAD, d), lambda bi, s: (bi, 0, 0))
    old = pl.BlockSpec((nb, tc * CHUNK, d), prv)
    vec = pl.BlockSpec((1, d), lambda bi, s: (0, 0))
    return pl.pallas_call(
        functools.partial(_wkv_tail_kernel, nb=nb, nchunk=tc, ngroup=d // GW, alpha=alpha),
        grid=(bsz // nb, steps + 1),
        in_specs=[seq, seq, mat, mat, st, old, old, old,
                  pl.BlockSpec((d, d), lambda bi, s: (0, 0), pipeline_mode=pl.Buffered(1)),
                  vec, vec, vec, vec],
        out_specs=[old, st],
        out_shape=[jax.ShapeDtypeStruct((bsz, t, d), F32), jax.ShapeDtypeStruct((bsz, HEAD, d), F32)],
        scratch_shapes=[pltpu.VMEM((nb, HEAD, d), F32), pltpu.VMEM((nb * tc * CHUNK, d), F32)],
        compiler_params=_cparams(("parallel", "arbitrary")),
        name="wkv_tail",
    )(qp, y0, m, dmat, h0, bonus, g, x, w_o, gn_g, gn_b, ln_g, ln_b)


def _glu_kernel(x_ref, wa_ref, wg_ref, ba_ref, bg_ref, u_ref, xb_ref):
    @pl.when(pl.program_id(1) == 0)
    def _():
        xb_ref[...] = x_ref[...].astype(BF16)

    xb = xb_ref[...]
    val = _dot(xb, wa_ref[...]) + ba_ref[...]
    gate = _dot(xb, wg_ref[...]) + bg_ref[...]
    u_ref[...] = val * _sigmoid(gate)


def _glu(x, w_in, b_in, tm, tn):
    rows, d = x.shape
    nj = d // tn
    return pl.pallas_call(
        _glu_kernel,
        grid=(rows // tm, nj),
        in_specs=[
            pl.BlockSpec((tm, d), lambda i, j: (i, 0)),
            pl.BlockSpec((d, tn), lambda i, j: (0, j)),
            pl.BlockSpec((d, tn), lambda i, j: (0, nj + j)),
            pl.BlockSpec((1, tn), lambda i, j: (0, j)),
            pl.BlockSpec((1, tn), lambda i, j: (0, nj + j)),
        ],
        out_specs=pl.BlockSpec((tm, tn), lambda i, j: (i, j)),
        out_shape=jax.ShapeDtypeStruct((rows, d), F32),
        scratch_shapes=[pltpu.VMEM((tm, d), BF16)],
        compiler_params=_cparams(("parallel", "arbitrary")),
        name="conv_glu",
    )(x, w_in, w_in, b_in, b_in)


def _conv_out_kernel(u_ref, halo_ref, st_ref, x_ref, dw_ref, dwb_ref, cg_ref, cb_ref, wo_ref, bo_ref,
                     lng_ref, lnb_ref, o_ref, win_ref, c_ref, *, alpha, tm):
    i = pl.program_id(1)
    d = u_ref.shape[2]

    @pl.when(i == 0)
    def _():
        win_ref[0:HALO, :] = st_ref[0]

    @pl.when(i > 0)
    def _():
        win_ref[0:HALO, :] = halo_ref[0]

    win_ref[HALO:HALO + tm, :] = u_ref[0]

    zrows = tm + 8
    for ls in range(0, d, GW):
        lanes = slice(ls, ls + GW)
        acc = jnp.zeros((tm, GW), F32)
        for s in range(8):
            z = None
            for q in range((HALO + 8) // 8):
                j = 8 * q + s - 2
                if j < 0 or j >= CONV_W:
                    continue
                nrows = tm if s == 0 else zrows
                term = win_ref[8 * q:8 * q + nrows, lanes] * dw_ref[j:j + 1, lanes]
                z = term if z is None else z + term
            if s == 0:
                acc = acc + z
            else:
                acc = acc + pltpu.roll(z, zrows - s, axis=0)[:tm]
        c_ref[:, lanes] = acc

    c = _layer_norm(c_ref[...] + dwb_ref[...], cg_ref[...], cb_ref[...])
    c = (c * _sigmoid(c)).astype(BF16)
    out = _dot(c, wo_ref[...]) + bo_ref[...]
    o_ref[0] = _layer_norm(alpha * x_ref[0] + out, lng_ref[...], lnb_ref[...])


def _conv_out(u, state_pad, x, dw, dw_b, cln_g, cln_b, w_out, b_out, ln_g, ln_b, alpha, tm):
    bsz, t, d = u.shape
    hb = tm // HALO
    tile = pl.BlockSpec((1, tm, d), lambda b, i: (b, i, 0))
    vec = pl.BlockSpec((1, d), lambda b, i: (0, 0))
    return pl.pallas_call(
        functools.partial(_conv_out_kernel, alpha=alpha, tm=tm),
        grid=(bsz, t // tm),
        in_specs=[
            tile,
            pl.BlockSpec((1, HALO, d), lambda b, i: (b, jnp.maximum(i * hb - 1, 0), 0)),
            pl.BlockSpec((1, HALO, d), lambda b, i: (b, 0, 0)),
            tile,
            pl.BlockSpec((HALO, d), lambda b, i: (0, 0)),
            vec, vec, vec,
            pl.BlockSpec((d, d), lambda b, i: (0, 0)),
            vec, vec, vec,
        ],
        out_specs=tile,
        out_shape=jax.ShapeDtypeStruct((bsz, t, d), F32),
        scratch_shapes=[pltpu.VMEM((HALO + tm, d), F32), pltpu.VMEM((tm, d), F32)],
        compiler_params=_cparams(("parallel", "arbitrary")),
        name="conv_out",
    )(u, u, state_pad, x, dw, dw_b, cln_g, cln_b, w_out, b_out, ln_g, ln_b)


def _pad_cols(w, n):
    return jnp.pad(w, ((0, 0), (0, n - w.shape[1])))


def _pad_rows(w, n):
    return jnp.pad(w, ((0, n - w.shape[0]), (0, 0)))


def _round_up(n, m):
    return -(-n // m) * m


def _prep_params(raw):
    depth = raw['w_up'].shape[0]
    n_rwkv = raw['w_r'].shape[0]
    n_conv = raw['c_w_in'].shape[0]
    row = lambda v: v.reshape(1, -1)
    rwkv = []
    for j in range(n_rwkv):
        lw = _round_up(raw['w1'].shape[2], 128)
        la = _round_up(raw['a1'].shape[2], 128)
        rwkv.append(dict(
            mu=_pad_rows(raw['mu'][j], 8),
            w_r=raw['w_r'][j].astype(BF16), w_k=raw['w_k'][j].astype(BF16), w_v=raw['w_v'][j].astype(BF16),
            w_o=raw['w_o'][j].astype(BF16),
            w1=_pad_cols(raw['w1'][j], lw).astype(BF16), w2=_pad_rows(raw['w2'][j], lw).astype(BF16),
            a1=_pad_cols(raw['a1'][j], la).astype(BF16), a2=_pad_rows(raw['a2'][j], la).astype(BF16),
            g1=raw['g1'][j].astype(BF16), g2=raw['g2'][j].astype(BF16),
            w0=row(raw['w0'][j]), a0=row(raw['a0'][j]), k_k=row(raw['k_k'][j]), k_a=row(raw['k_a'][j]),
            r_k=row(raw['r_k'][j]), gn_g=row(raw['gn_g'][j]), gn_b=row(raw['gn_b'][j])))
    conv = []
    for j in range(n_conv):
        conv.append(dict(
            w_in=raw['c_w_in'][j].astype(BF16), b_in=row(raw['c_b_in'][j]),
            dw=_pad_rows(raw['c_dw'][j], HALO), dw_b=row(raw['c_dw_b'][j]),
            ln_g=row(raw['c_ln_g'][j]), ln_b=row(raw['c_ln_b'][j]),
            w_out=raw['c_w_out'][j].astype(BF16), b_out=row(raw['c_b_out'][j])))
    mlp = [dict(w_up=raw['w_up'][i].astype(BF16), w_down=raw['w_down'][i].astype(BF16),
                mix_g=row(raw['ln_mix_g'][i]), mix_b=row(raw['ln_mix_b'][i]),
                ffn_g=row(raw['ln_ffn_g'][i]), ffn_b=row(raw['ln_ffn_b'][i])) for i in range(depth)]
    return rwkv, conv, mlp


def _rwkv_layer(x, shift, wkv, p, lnp, alpha, cfg):
    bsz, t, d = x.shape
    r, lw, k, v, kap, b, g, bonus = _rwkv_proj(x, shift, p, cfg['tm_seq'], cfg['tn'])
    tp = _round_up(t, CHUNK)
    xp = x
    if tp != t:
        pad = lambda a: jnp.pad(a, ((0, 0), (0, tp - t), (0, 0)))
        r, lw, k, v, kap, b, g, bonus, xp = [pad(a) for a in (r, lw, k, v, kap, b, g, bonus, x)]
    qp, y0, m, dmat = _wkv_pre(r, lw, k, v, kap, b, cfg['tc_pre'], cfg['tg_pre'])
    nh = d // HEAD
    h0 = jnp.transpose(wkv, (0, 3, 1, 2)).reshape(bsz, HEAD, d)
    x1, h1 = _wkv_tail(qp, y0, m, dmat, h0, bonus, g, xp, p['w_o'], p['gn_g'], p['gn_b'],
                       lnp['mix_g'], lnp['mix_b'], alpha, cfg['nb_seq'], cfg['tc_seq'])
    new_wkv = jnp.transpose(h1.reshape(bsz, HEAD, nh, HEAD), (0, 2, 3, 1))
    return x1[:, :t], x[:, -1], new_wkv


def _conv_layer(x, state, p, lnp, alpha, cfg):
    bsz, t, d = x.shape
    u = _glu(x.reshape(bsz * t, d), p['w_in'], p['b_in'], cfg['tm_rows'], cfg['tn']).reshape(bsz, t, d)
    state_pad = jnp.pad(state, ((0, 0), (HALO - CONV_STATE, 0), (0, 0)))
    x1 = _conv_out(u, state_pad, x, p['dw'], p['dw_b'], p['ln_g'], p['ln_b'], p['w_out'], p['b_out'],
                   lnp['mix_g'], lnp['mix_b'], alpha, cfg['tm_conv'])
    keep = min(t, CONV_STATE)
    new_state = jnp.concatenate([state[:, keep:], u[:, t - keep:]], axis=1)
    return x1, new_state


def _run_trunk(x, st_shift, st_wkv, st_conv, params, cfg):
    rwkv, conv, mlp = params
    depth = len(mlp)
    alpha = (2.0 * depth) ** 0.25
    bsz, t, d = x.shape
    new_shift, new_wkv, new_conv = [], [], []
    for i in range(depth):
        j = i // 2
        if i % 2 == 0:
            x, sh, s = _rwkv_layer(x, st_shift[j], st_wkv[j], rwkv[j], mlp[i], alpha, cfg)
            new_shift.append(sh)
            new_wkv.append(s)
        else:
            x, cs = _conv_layer(x, st_conv[j], conv[j], mlp[i], alpha, cfg)
            new_conv.append(cs)
        x = _mlp(x.reshape(bsz * t, d), mlp[i]['w_up'], mlp[i]['w_down'], mlp[i]['ffn_g'], mlp[i]['ffn_b'],
                 alpha, cfg['tm_rows'], cfg['tf']).reshape(bsz, t, d)
    return x, jnp.stack(new_wkv), jnp.stack(new_shift), jnp.stack(new_conv)


def _config(bsz, t):
    rows = bsz * t
    tm_seq = min(t, 512)
    nchunk = _round_up(t, CHUNK) // CHUNK
    tc_pre = min(nchunk, WKV_UNITS)
    return dict(
        tm_seq=tm_seq, tn=512, tm_rows=min(rows, 512), tf=1024,
        tm_conv=min(t, 256), tc_pre=tc_pre, tg_pre=WKV_UNITS // tc_pre, tc_seq=min(nchunk, 2), nb_seq=min(bsz, 2))


def kernel(x_prompt, x_sample, state_wkv, state_shift, state_conv, mu, w_r, w_k, w_v, w_o, w0, w1, w2, a0, a1, a2, g1, g2, k_k, k_a, r_k, gn_g, gn_b, c_w_in, c_b_in, c_dw, c_dw_b, c_ln_g, c_ln_b, c_w_out, c_b_out, w_up, w_down, ln_mix_g, ln_mix_b, ln_ffn_g, ln_ffn_b):
    raw = dict(mu=mu, w_r=w_r, w_k=w_k, w_v=w_v, w_o=w_o, w0=w0, w1=w1, w2=w2, a0=a0, a1=a1, a2=a2,
               g1=g1, g2=g2, k_k=k_k, k_a=k_a, r_k=r_k, gn_g=gn_g, gn_b=gn_b, c_w_in=c_w_in,
               c_b_in=c_b_in, c_dw=c_dw, c_dw_b=c_dw_b, c_ln_g=c_ln_g, c_ln_b=c_ln_b, c_w_out=c_w_out,
               c_b_out=c_b_out, w_up=w_up, w_down=w_down, ln_mix_g=ln_mix_g, ln_mix_b=ln_mix_b,
               ln_ffn_g=ln_ffn_g, ln_ffn_b=ln_ffn_b)
    params = _prep_params(raw)
    bp, tp, d = x_prompt.shape
    bs, ts, _ = x_sample.shape
    n_rwkv, n_conv = w_r.shape[0], c_w_in.shape[0]
    nh = d // HEAD
    z_wkv = jnp.zeros((n_rwkv, bp, nh, HEAD, HEAD), F32)
    z_shift = jnp.zeros((n_rwkv, bp, d), F32)
    z_conv = jnp.zeros((n_conv, bp, CONV_STATE, d), F32)
    y_p, wkv_p, shift_p, conv_p = _run_trunk(x_prompt, z_shift, z_wkv, z_conv, params, _config(bp, tp))
    y_s, wkv_s, shift_s, conv_s = _run_trunk(x_sample, state_shift, state_wkv, state_conv, params, _config(bs, ts))
    return (y_p, y_s, wkv_p, shift_p, conv_p, wkv_s, shift_s, conv_s)
```

```python
import functools
import math

import jax
import jax.numpy as jnp
from jax import lax
from jax.experimental import pallas as pl
from jax.experimental.pallas import tpu as pltpu

F32 = jnp.float32
BF16 = jnp.bfloat16

SUBLANE = 8
HEAD = 64
GROUP = 4
GW = HEAD * GROUP
CHUNK = 64
WKV_UNITS = 8
CONV_W = 31
CONV_STATE = CONV_W - 1
HALO = 32
LN_EPS = 1e-5
GN_EPS = 64e-5
VMEM_LIMIT = 56 * 1024 * 1024

NN = ((1,), (0,))
NT = ((1,), (1,))


def _cparams(sem):
    return pltpu.CompilerParams(dimension_semantics=sem, vmem_limit_bytes=VMEM_LIMIT)


def _dot(a, b, dims=NN):
    return lax.dot_general(a, b, (dims, ((), ())), preferred_element_type=F32)


def _split(x):
    hi = x.astype(BF16)
    lo = (x - hi.astype(F32)).astype(BF16)
    return hi, lo


def _layer_norm(x, g, b):
    mu = jnp.mean(x, axis=-1, keepdims=True)
    xc = x - mu
    var = jnp.mean(xc * xc, axis=-1, keepdims=True)
    return xc * lax.rsqrt(var + LN_EPS) * g + b


def _sigmoid(x):
    return 1.0 / (1.0 + jnp.exp(-x))


def _lane_head(shape, dim):
    return lax.broadcasted_iota(jnp.int32, shape, dim) // HEAD


def _block_diag(x):
    head = _lane_head(x.shape, 1)
    return jnp.concatenate([jnp.where(head == h, x, 0.0) for h in range(GROUP)], axis=0)


def _head_ones():
    return jnp.where(_lane_head((GW, GW), 0) == _lane_head((GW, GW), 1), 1.0, 0.0).astype(BF16)


def _head_sum(x, ones):
    hi, lo = _split(x)
    return _dot(hi, ones) + _dot(lo, ones)


def _mlp_kernel(x_ref, wu_ref, wd_ref, g_ref, b_ref, o_ref, acc_ref, xb_ref, *, alpha):
    j = pl.program_id(1)

    @pl.when(j == 0)
    def _():
        xb_ref[...] = x_ref[...].astype(BF16)
        acc_ref[...] = jnp.zeros_like(acc_ref)

    h = jnp.maximum(_dot(xb_ref[...], wu_ref[...]), 0.0)
    acc_ref[...] += _dot((h * h).astype(BF16), wd_ref[...])

    @pl.when(j == pl.num_programs(1) - 1)
    def _():
        o_ref[...] = _layer_norm(alpha * x_ref[...] + acc_ref[...], g_ref[...], b_ref[...])


def _mlp(x, w_up, w_down, g, b, alpha, tm, tf):
    rows, d = x.shape
    dff = w_up.shape[1]
    return pl.pallas_call(
        functools.partial(_mlp_kernel, alpha=alpha),
        grid=(rows // tm, dff // tf),
        in_specs=[
            pl.BlockSpec((tm, d), lambda i, j: (i, 0)),
            pl.BlockSpec((d, tf), lambda i, j: (0, j)),
            pl.BlockSpec((tf, d), lambda i, j: (j, 0)),
            pl.BlockSpec((1, d), lambda i, j: (0, 0)),
            pl.BlockSpec((1, d), lambda i, j: (0, 0)),
        ],
        out_specs=pl.BlockSpec((tm, d), lambda i, j: (i, 0)),
        out_shape=jax.ShapeDtypeStruct((rows, d), F32),
        scratch_shapes=[pltpu.VMEM((tm, d), F32), pltpu.VMEM((tm, d), BF16)],
        compiler_params=_cparams(("parallel", "arbitrary")),
        name="mlp",
    )(x, w_up, w_down, g, b)


def _rwkv_proj_kernel(x_ref, tail_ref, shift_ref, mu_ref, w1_ref, a1_ref, g1_ref,
                      wr_ref, wk_ref, wv_ref, w2_ref, a2_ref, g2_ref,
                      w0_ref, a0_ref, kk_ref, ka_ref, rk_ref,
                      r_o, lw_o, k_o, v_o, kap_o, b_o, g_o, bonus_o, last_o,
                      xm_ref, hw_ref, ha_ref, hg_ref):
    j = pl.program_id(2)

    @pl.when(j == 0)
    def _():
        x = x_ref[0]
        rolled = pltpu.roll(x, 1, axis=0)
        first = lax.broadcasted_iota(jnp.int32, x.shape, 0) == 0
        before = jnp.where(pl.program_id(1) == 0, shift_ref[0], tail_ref[0, SUBLANE - 1:SUBLANE, :])
        xx = jnp.where(first, before, rolled) - x
        last_o[0] = x[x.shape[0] - SUBLANE:, :]
        mu = mu_ref[...]
        xm_ref[0] = (x + xx * mu[0:1]).astype(BF16)
        xm_ref[1] = (x + xx * mu[2:3]).astype(BF16)
        xm_ref[2] = (x + xx * mu[3:4]).astype(BF16)
        xw = (x + xx * mu[1:2]).astype(BF16)
        xa = (x + xx * mu[4:5]).astype(BF16)
        xg = (x + xx * mu[5:6]).astype(BF16)
        hw_ref[...] = jnp.tanh(_dot(xw, w1_ref[...])).astype(BF16)
        ha_ref[...] = _dot(xa, a1_ref[...]).astype(BF16)
        hg_ref[...] = _sigmoid(_dot(xg, g1_ref[...])).astype(BF16)

    r = _dot(xm_ref[0], wr_ref[...])
    k = _dot(xm_ref[1], wk_ref[...])
    v = _dot(xm_ref[2], wv_ref[...])
    z = -(w0_ref[...] + _dot(hw_ref[...], w2_ref[...]))
    w_log = -(jnp.maximum(z, 0.0) + jnp.log(1.0 + jnp.exp(-jnp.abs(z)))) - 0.5
    a = _sigmoid(a0_ref[...] + _dot(ha_ref[...], a2_ref[...]))
    g = _dot(hg_ref[...], g2_ref[...])

    kk = k * kk_ref[...]
    ones = _head_ones()
    slabs = range(0, kk.shape[1], GW)
    sq = kk * kk
    ss = jnp.concatenate([_head_sum(sq[:, s:s + GW], ones) for s in slabs], axis=1)
    kap = kk * lax.rsqrt(jnp.maximum(ss, 1e-24))
    k2 = k * (1.0 + (a - 1.0) * ka_ref[...])
    rkk = r * k2 * rk_ref[...]
    bonus = jnp.concatenate([_head_sum(rkk[:, s:s + GW], ones) for s in slabs], axis=1) * v

    r_o[0] = r
    lw_o[0] = -jnp.exp(w_log)
    k_o[0] = k2
    v_o[0] = v
    kap_o[0] = kap
    b_o[0] = kap * a
    g_o[0] = g
    bonus_o[0] = bonus


def _rwkv_proj(x, shift, p, tm, tn):
    bsz, t, d = x.shape
    nt = t // tm
    tail_blocks = tm // SUBLANE
    lw, la, lg = p['w1'].shape[1], p['a1'].shape[1], p['g1'].shape[1]
    full = lambda shape: pl.BlockSpec(shape, lambda b, i, j: (0,) * len(shape))
    col = lambda rows: pl.BlockSpec((rows, tn), lambda b, i, j: (0, j))
    out_spec = pl.BlockSpec((1, tm, tn), lambda b, i, j: (b, i, j))
    out_sds = jax.ShapeDtypeStruct((bsz, t, d), F32)
    return pl.pallas_call(
        _rwkv_proj_kernel,
        grid=(bsz, nt, d // tn),
        in_specs=[
            pl.BlockSpec((1, tm, d), lambda b, i, j: (b, i, 0)),
            pl.BlockSpec((1, SUBLANE, d), lambda b, i, j: (b, jnp.maximum(i * tail_blocks - 1, 0), 0)),
            pl.BlockSpec((1, 1, d), lambda b, i, j: (b, 0, 0)),
            full((8, d)), full((d, lw)), full((d, la)), full((d, lg)),
            col(d), col(d), col(d), col(lw), col(la), col(lg),
            col(1), col(1), col(1), col(1), col(1),
        ],
        out_specs=[out_spec] * 8 + [pl.BlockSpec((1, SUBLANE, d), lambda b, i, j: (b, 0, 0))],
        out_shape=[out_sds] * 8 + [jax.ShapeDtypeStruct((bsz, SUBLANE, d), F32)],
        scratch_shapes=[pltpu.VMEM((3, tm, d), BF16), pltpu.VMEM((tm, lw), BF16),
                        pltpu.VMEM((tm, la), BF16), pltpu.VMEM((tm, lg), BF16)],
        compiler_params=_cparams(("parallel", "arbitrary", "arbitrary")),
        name="rwkv_proj",
    )(x, x, shift[:, None, :], p['mu'], p['w1'], p['a1'], p['g1'], p['w_r'], p['w_k'], p['w_v'],
      p['w2'], p['a2'], p['g2'], p['w0'], p['a0'], p['k_k'], p['k_a'], p['r_k'])


def _head_transpose(x):
    xt = x.T
    return jnp.concatenate([xt[h * HEAD:(h + 1) * HEAD, :] for h in range(GROUP)], axis=1)


def _b16(x):
    return x.astype(BF16)


def _each(f, *lists):
    return [f(*args) for args in zip(*lists)]


def _bd16(x):
    return _b16(_block_diag(x))


def _wkv_chunks(r, lw, k, v, kap, b):
    c = CHUNK
    row = lax.broadcasted_iota(jnp.int32, (c, GW), 0)
    pos = lax.broadcasted_iota(jnp.int32, (c, GW), 1) % HEAD
    tri = jnp.where(lax.broadcasted_iota(jnp.int32, (c, c), 1) <= lax.broadcasted_iota(jnp.int32, (c, c), 0),
                    1.0, 0.0).astype(BF16)
    strict = pos < row
    lower = pos <= row
    diag = pos == row

    def cumsum(x):
        hi = _b16(x)
        rem = x - hi.astype(F32)
        mid = _b16(rem)
        lo = _b16(rem - mid.astype(F32))
        return _dot(tri, hi) + (_dot(tri, mid) + _dot(tri, lo))

    cum = _each(cumsum, lw)
    end = _each(lambda s: s[c - 1:c, :], cum)
    e_neg = _each(lambda s: jnp.exp(-s), cum)
    e_end = _each(lambda e, s: jnp.exp(e - s), end, cum)
    rt = _each(lambda x, s: x * jnp.exp(s), r, cum)
    kq = _each(lambda x, s, l: x * jnp.exp(s - l), kap, cum, lw)
    bh = _each(jnp.multiply, b, e_neg)
    kh = _each(jnp.multiply, k, e_neg)
    bp = _each(jnp.multiply, b, e_end)
    kp = _each(jnp.multiply, k, e_end)

    lhs = _each(lambda x, y: _b16(jnp.concatenate([x, y], axis=0)), kq, rt)
    gb = _each(lambda x, y: _dot(x, _bd16(y), NT), lhs, bh)
    gk = _each(lambda x, y: _dot(x, _bd16(y), NT), lhs, kh)
    l_ub = _each(lambda g: jnp.where(strict, g[:c], 0.0), gb)
    a_uk = _each(lambda g: jnp.where(strict, g[:c], 0.0), gk)
    a_rb = _each(lambda g: jnp.where(lower, g[c:], 0.0), gb)
    a_rk = _each(lambda g: jnp.where(lower, g[c:], 0.0), gk)

    stack = lambda x, y: _b16(jnp.concatenate([x, y], axis=0))
    nm = _each(jnp.negative, l_ub)
    pw = _each(lambda x: _dot(_b16(x), _bd16(x)), l_ub)
    for _ in range(int(math.log2(c)) - 2):
        res = _each(lambda x, n: _dot(stack(x, n), _bd16(x)), pw, nm)
        nm = _each(lambda n, x, y: n + x + y[c:], nm, pw, res)
        pw = _each(lambda y: y[:c], res)
    nm = _each(lambda n, x: n + x + _dot(_b16(n), _bd16(x)), nm, pw)
    nm16 = _each(_b16, nm)

    bpx = _each(lambda x: _b16(_head_transpose(x)), bp)
    kpx = _each(lambda x: _b16(_head_transpose(x)), kp)
    wa = _each(lambda x, y, z, w: _dot(jnp.concatenate([stack(x, y), z], axis=0), _bd16(w)), a_uk, a_rk, kpx, v)
    p = _each(lambda x, n: x + _dot(n, _bd16(x)), kq, nm16)
    u0 = _each(lambda x, n: -(x[:c] + _dot(n, _bd16(x[:c]))), wa, nm16)

    lhs2 = _each(lambda a, x: jnp.concatenate([_b16(a), x], axis=0), a_rb, bpx)
    rp = _each(lambda a, y: _dot(a, _bd16(y)), lhs2, p)
    ru = _each(lambda a, y: _dot(a, _bd16(y)), lhs2, u0)
    qp = _each(lambda x, y: x - y[:c], rt, rp)
    y0 = _each(lambda y, x: y[:c] + x[c:2 * c], ru, wa)
    m_cat = _each(lambda e, y: jnp.where(diag, jnp.exp(e), 0.0) - y[c:], end, rp)
    d_cat = _each(lambda y, x: y[c:] + x[2 * c:], ru, wa)
    return qp, y0, m_cat, d_cat


def _wkv_pre_kernel(r_ref, lw_ref, k_ref, v_ref, kap_ref, b_ref, qp_o, y0_o, m_o, d_o, *, nchunk, ngroup):
    units = [(ci, slice(ci * CHUNK, (ci + 1) * CHUNK), slice(gi * GW, (gi + 1) * GW))
             for ci in range(nchunk) for gi in range(ngroup)]
    load = lambda ref: [ref[0, rs, ls] for _, rs, ls in units]
    qp, y0, m_cat, d_cat = _wkv_chunks(load(r_ref), load(lw_ref), load(k_ref), load(v_ref),
                                       load(kap_ref), load(b_ref))
    for n, (ci, rs, ls) in enumerate(units):
        qp_o[0, rs, ls] = qp[n]
        y0_o[0, rs, ls] = y0[n]
        m_o[0, ci, :, ls] = m_cat[n]
        d_o[0, ci, :, ls] = d_cat[n]


def _wkv_pre(r, lw, k, v, kap, b, tc, tg):
    bsz, t, d = r.shape
    nc = t // CHUNK
    seq = pl.BlockSpec((1, tc * CHUNK, tg * GW), lambda bi, ci, gi: (bi, ci, gi))
    mat = pl.BlockSpec((1, tc, HEAD, tg * GW), lambda bi, ci, gi: (bi, ci, 0, gi))
    seq_sds = jax.ShapeDtypeStruct((bsz, t, d), F32)
    mat_sds = jax.ShapeDtypeStruct((bsz, nc, HEAD, d), F32)
    return pl.pallas_call(
        functools.partial(_wkv_pre_kernel, nchunk=tc, ngroup=tg),
        grid=(bsz, nc // tc, d // (tg * GW)),
        in_specs=[seq] * 6,
        out_specs=[seq, seq, mat, mat],
        out_shape=[seq_sds, seq_sds, mat_sds, mat_sds],
        compiler_params=_cparams(("parallel", "parallel", "parallel")),
        name="wkv_pre",
    )(r, lw, k, v, kap, b)


def _wkv_tail_kernel(qp_ref, y0_ref, m_ref, d_ref, h0_ref, bonus_ref, g_ref, x_ref, wo_ref,
                     gng_ref, gnb_ref, lng_ref, lnb_ref, o_ref, h_o, h_ref, y_ref,
                     *, nb, nchunk, ngroup, alpha):
    s = pl.program_id(1)
    rows_b = nchunk * CHUNK

    @pl.when(s == 0)
    def _():
        h_ref[...] = h0_ref[...]
        y_ref[...] = jnp.zeros_like(y_ref)

    ones = _head_ones()
    y = y_ref[...]
    parts = []
    for ls in range(0, y.shape[1], GW):
        ys = y[:, ls:ls + GW]
        mean = _dot(_b16(ys), ones) * (1.0 / HEAD)
        yc = ys - mean
        var = _dot(_b16(yc * yc), ones) * (1.0 / HEAD)
        parts.append(yc * lax.rsqrt(var + GN_EPS))
    yn = jnp.concatenate(parts, axis=1) * gng_ref[...] + gnb_ref[...]
    rows_of = lambda ref: jnp.concatenate([ref[bi] for bi in range(nb)], axis=0)
    z = _b16((yn + rows_of(bonus_ref)) * rows_of(g_ref))
    res = _layer_norm(alpha * rows_of(x_ref) + _dot(z, wo_ref[...]), lng_ref[...], lnb_ref[...])
    for bi in range(nb):
        o_ref[bi] = res[bi * rows_b:(bi + 1) * rows_b]

    for c in range(nchunk):
        rows = slice(c * CHUNK, (c + 1) * CHUNK)
        for bi in range(nb):
            for gi in range(ngroup):
                lanes = slice(gi * GW, (gi + 1) * GW)
                h_hi, h_lo = _split(_block_diag(h_ref[bi, :, lanes]))
                m_hi, m_lo = _split(m_ref[bi, c, :, lanes])
                lhs = jnp.concatenate([_b16(qp_ref[bi, rows, lanes]), m_hi, m_lo], axis=0)
                out = _dot(lhs, h_hi)
                y_ref[bi * rows_b + c * CHUNK:bi * rows_b + (c + 1) * CHUNK, lanes] = (
                    out[:CHUNK] + y0_ref[bi, rows, lanes])
                h_ref[bi, :, lanes] = (out[CHUNK:2 * CHUNK] + (out[2 * CHUNK:] + _dot(m_hi, h_lo))
                                       + d_ref[bi, c, :, lanes])

    @pl.when(s == pl.num_programs(1) - 2)
    def _():
        h_o[...] = h_ref[...]


def _wkv_tail(qp, y0, m, dmat, h0, bonus, g, x, w_o, gn_g, gn_b, ln_g, ln_b, alpha, nb, tc):
    bsz, t, d = qp.shape
    steps = t // (tc * CHUNK)
    cur = lambda bi, s: (bi, jnp.minimum(s, steps - 1), 0)
    prv = lambda bi, s: (bi, jnp.maximum(s - 1, 0), 0)
    seq = pl.BlockSpec((nb, tc * CHUNK, d), cur)
    mat = pl.BlockSpec((nb, tc, HEAD, d), lambda bi, s: (bi, jnp.minimum(s, steps - 1), 0, 0))
    st = pl.BlockSpec((nb, HEAD, d), lambda bi, s: (bi, 0, 0))
    old = pl.BlockSpec((nb, tc * CHUNK, d), prv)
    vec = pl.BlockSpec((1, d), lambda bi, s: (0, 0))
    return pl.pallas_call(
        functools.partial(_wkv_tail_kernel, nb=nb, nchunk=tc, ngroup=d // GW, alpha=alpha),
        grid=(bsz // nb, steps + 1),
        in_specs=[seq, seq, mat, mat, st, old, old, old,
                  pl.BlockSpec((d, d), lambda bi, s: (0, 0), pipeline_mode=pl.Buffered(1)),
                  vec, vec, vec, vec],
        out_specs=[old, st],
        out_shape=[jax.ShapeDtypeStruct((bsz, t, d), F32), jax.ShapeDtypeStruct((bsz, HEAD, d), F32)],
        scratch_shapes=[pltpu.VMEM((nb, HEAD, d), F32), pltpu.VMEM((nb * tc * CHUNK, d), F32)],
        compiler_params=_cparams(("parallel", "arbitrary")),
        name="wkv_tail",
    )(qp, y0, m, dmat, h0, bonus, g, x, w_o, gn_g, gn_b, ln_g, ln_b)


def _conv_kernel(xc_ref, xp_ref, st_ref, win_w, bin_ref, dw_ref, dwb_ref, cg_ref, cb_ref, wo_ref, bo_ref,
                 lng_ref, lnb_ref, o_ref, st_o, win_ref, u_ref, c_ref, *, alpha, tm, tn):
    step = pl.program_id(1)
    d = xc_ref.shape[2]

    @pl.when(step == 0)
    def _():
        win_ref[...] = jnp.zeros_like(win_ref)

    zrows = tm + 8
    for ls in range(0, d, GW):
        lanes = slice(ls, ls + GW)
        acc = jnp.zeros((tm, GW), F32)
        for s in range(8):
            z = None
            for q in range((HALO + 8) // 8):
                j = 8 * q + s - 2
                if j < 0 or j >= CONV_W:
                    continue
                nrows = tm if s == 0 else zrows
                term = win_ref[8 * q:8 * q + nrows, lanes] * dw_ref[j:j + 1, lanes]
                z = term if z is None else z + term
            if s == 0:
                acc = acc + z
            else:
                acc = acc + pltpu.roll(z, zrows - s, axis=0)[:tm]
        c_ref[:, lanes] = acc

    c = _layer_norm(c_ref[...] + dwb_ref[...], cg_ref[...], cb_ref[...])
    c = (c * _sigmoid(c)).astype(BF16)
    out = _dot(c, wo_ref[...]) + bo_ref[...]
    o_ref[0] = _layer_norm(alpha * xp_ref[0] + out, lng_ref[...], lnb_ref[...])

    xb = _b16(xc_ref[0])
    for cs in range(0, d, tn):
        val = _dot(xb, win_w[:, cs:cs + tn]) + bin_ref[:, cs:cs + tn]
        gate = _dot(xb, win_w[:, d + cs:d + cs + tn]) + bin_ref[:, d + cs:d + cs + tn]
        u_ref[:, cs:cs + tn] = val * _sigmoid(gate)

    win_ref[0:HALO, :] = jnp.where(step == 0, st_ref[0], win_ref[tm:tm + HALO, :])
    win_ref[HALO:HALO + tm, :] = u_ref[...]

    @pl.when(step == pl.num_programs(1) - 2)
    def _():
        st_o[0] = u_ref[tm - HALO:tm, :]


def _conv(x, state_pad, w_in, b_in, dw, dw_b, cln_g, cln_b, w_out, b_out, ln_g, ln_b, alpha, tm, tn):
    bsz, t, d = x.shape
    steps = t // tm
    vec = pl.BlockSpec((1, d), lambda b, s: (0, 0))
    once = lambda shape: pl.BlockSpec(shape, lambda b, s: (0, 0), pipeline_mode=pl.Buffered(1))
    hist = pl.BlockSpec((1, HALO, d), lambda b, s: (b, 0, 0))
    prev_tile = pl.BlockSpec((1, tm, d), lambda b, s: (b, jnp.maximum(s - 1, 0), 0))
    return pl.pallas_call(
        functools.partial(_conv_kernel, alpha=alpha, tm=tm, tn=tn),
        grid=(bsz, steps + 1),
        in_specs=[
            pl.BlockSpec((1, tm, d), lambda b, s: (b, jnp.minimum(s, steps - 1), 0)),
            prev_tile, hist,
            once((d, 2 * d)), pl.BlockSpec((1, 2 * d), lambda b, s: (0, 0)),
            pl.BlockSpec((HALO, d), lambda b, s: (0, 0)),
            vec, vec, vec,
            once((d, d)),
            vec, vec, vec,
        ],
        out_specs=[prev_tile, hist],
        out_shape=[jax.ShapeDtypeStruct((bsz, t, d), F32), jax.ShapeDtypeStruct((bsz, HALO, d), F32)],
        scratch_shapes=[pltpu.VMEM((HALO + tm, d), F32), pltpu.VMEM((tm, d), F32), pltpu.VMEM((tm, d), F32)],
        compiler_params=_cparams(("parallel", "arbitrary")),
        name="conv",
    )(x, x, state_pad, w_in, b_in, dw, dw_b, cln_g, cln_b, w_out, b_out, ln_g, ln_b)


def _pad_cols(w, n):
    return jnp.pad(w, ((0, 0), (0, n - w.shape[1])))


def _pad_rows(w, n):
    return jnp.pad(w, ((0, n - w.shape[0]), (0, 0)))


def _round_up(n, m):
    return -(-n // m) * m


def _prep_params(raw):
    depth = raw['w_up'].shape[0]
    n_rwkv = raw['w_r'].shape[0]
    n_conv = raw['c_w_in'].shape[0]
    row = lambda v: v.reshape(1, -1)
    rwkv = []
    for j in range(n_rwkv):
        lw = _round_up(raw['w1'].shape[2], 128)
        la = _round_up(raw['a1'].shape[2], 128)
        rwkv.append(dict(
            mu=_pad_rows(raw['mu'][j], 8),
            w_r=raw['w_r'][j].astype(BF16), w_k=raw['w_k'][j].astype(BF16), w_v=raw['w_v'][j].astype(BF16),
            w_o=raw['w_o'][j].astype(BF16),
            w1=_pad_cols(raw['w1'][j], lw).astype(BF16), w2=_pad_rows(raw['w2'][j], lw).astype(BF16),
            a1=_pad_cols(raw['a1'][j], la).astype(BF16), a2=_pad_rows(raw['a2'][j], la).astype(BF16),
            g1=raw['g1'][j].astype(BF16), g2=raw['g2'][j].astype(BF16),
            w0=row(raw['w0'][j]), a0=row(raw['a0'][j]), k_k=row(raw['k_k'][j]), k_a=row(raw['k_a'][j]),
            r_k=row(raw['r_k'][j]), gn_g=row(raw['gn_g'][j]), gn_b=row(raw['gn_b'][j])))
    conv = []
    for j in range(n_conv):
        conv.append(dict(
            w_in=raw['c_w_in'][j].astype(BF16), b_in=row(raw['c_b_in'][j]),
            dw=_pad_rows(raw['c_dw'][j], HALO), dw_b=row(raw['c_dw_b'][j]),
            ln_g=row(raw['c_ln_g'][j]), ln_b=row(raw['c_ln_b'][j]),
            w_out=raw['c_w_out'][j].astype(BF16), b_out=row(raw['c_b_out'][j])))
    mlp = [dict(w_up=raw['w_up'][i].astype(BF16), w_down=raw['w_down'][i].astype(BF16),
                mix_g=row(raw['ln_mix_g'][i]), mix_b=row(raw['ln_mix_b'][i]),
                ffn_g=row(raw['ln_ffn_g'][i]), ffn_b=row(raw['ln_ffn_b'][i])) for i in range(depth)]
    return rwkv, conv, mlp


def _rwkv_layer(x, shift, wkv, p, lnp, alpha, cfg):
    bsz, t, d = x.shape
    r, lw, k, v, kap, b, g, bonus, last = _rwkv_proj(x, shift, p, cfg['tm_seq'], cfg['tn'])
    tp = _round_up(t, CHUNK)
    xp = x
    if tp != t:
        pad = lambda a: jnp.pad(a, ((0, 0), (0, tp - t), (0, 0)))
        r, lw, k, v, kap, b, g, bonus, xp = [pad(a) for a in (r, lw, k, v, kap, b, g, bonus, x)]
    qp, y0, m, dmat = _wkv_pre(r, lw, k, v, kap, b, cfg['tc_pre'], cfg['tg_pre'])
    nh = d // HEAD
    h0 = jnp.transpose(wkv, (0, 3, 1, 2)).reshape(bsz, HEAD, d)
    x1, h1 = _wkv_tail(qp, y0, m, dmat, h0, bonus, g, xp, p['w_o'], p['gn_g'], p['gn_b'],
                       lnp['mix_g'], lnp['mix_b'], alpha, cfg['nb_seq'], cfg['tc_seq'])
    new_wkv = jnp.transpose(h1.reshape(bsz, HEAD, nh, HEAD), (0, 2, 3, 1))
    return x1[:, :t], last[:, SUBLANE - 1], new_wkv


def _conv_layer(x, state, p, lnp, alpha, cfg):
    bsz, t, d = x.shape
    assert t >= HALO
    state_pad = jnp.pad(state, ((0, 0), (HALO - CONV_STATE, 0), (0, 0)))
    x1, hist = _conv(x, state_pad, p['w_in'], p['b_in'], p['dw'], p['dw_b'], p['ln_g'], p['ln_b'],
                     p['w_out'], p['b_out'], lnp['mix_g'], lnp['mix_b'], alpha, cfg['tm_conv'], cfg['tn'])
    return x1, hist[:, HALO - CONV_STATE:]


def _run_trunk(x, st_shift, st_wkv, st_conv, params, cfg):
    rwkv, conv, mlp = params
    depth = len(mlp)
    alpha = (2.0 * depth) ** 0.25
    bsz, t, d = x.shape
    new_shift, new_wkv, new_conv = [], [], []
    for i in range(depth):
        j = i // 2
        if i % 2 == 0:
            x, sh, s = _rwkv_layer(x, st_shift[j], st_wkv[j], rwkv[j], mlp[i], alpha, cfg)
            new_shift.append(sh)
            new_wkv.append(s)
        else:
            x, cs = _conv_layer(x, st_conv[j], conv[j], mlp[i], alpha, cfg)
            new_conv.append(cs)
        x = _mlp(x.reshape(bsz * t, d), mlp[i]['w_up'], mlp[i]['w_down'], mlp[i]['ffn_g'], mlp[i]['ffn_b'],
                 alpha, cfg['tm_rows'], cfg['tf']).reshape(bsz, t, d)
    return x, jnp.stack(new_wkv), jnp.stack(new_shift), jnp.stack(new_conv)


def _config(bsz, t):
    rows = bsz * t
    tm_seq = min(t, 512)
    nchunk = _round_up(t, CHUNK) // CHUNK
    tc_pre = min(nchunk, WKV_UNITS)
    return dict(
        tm_seq=tm_seq, tn=512, tm_rows=min(rows, 512), tf=1024,
        tm_conv=min(t, 256), tc_pre=tc_pre, tg_pre=WKV_UNITS // tc_pre, tc_seq=min(nchunk, 2), nb_seq=min(bsz, 2))


def kernel(x_prompt, x_sample, state_wkv, state_shift, state_conv, mu, w_r, w_k, w_v, w_o, w0, w1, w2, a0, a1, a2, g1, g2, k_k, k_a, r_k, gn_g, gn_b, c_w_in, c_b_in, c_dw, c_dw_b, c_ln_g, c_ln_b, c_w_out, c_b_out, w_up, w_down, ln_mix_g, ln_mix_b, ln_ffn_g, ln_ffn_b):
    raw = dict(mu=mu, w_r=w_r, w_k=w_k, w_v=w_v, w_o=w_o, w0=w0, w1=w1, w2=w2, a0=a0, a1=a1, a2=a2,
               g1=g1, g2=g2, k_k=k_k, k_a=k_a, r_k=r_k, gn_g=gn_g, gn_b=gn_b, c_w_in=c_w_in,
               c_b_in=c_b_in, c_dw=c_dw, c_dw_b=c_dw_b, c_ln_g=c_ln_g, c_ln_b=c_ln_b, c_w_out=c_w_out,
               c_b_out=c_b_out, w_up=w_up, w_down=w_down, ln_mix_g=ln_mix_g, ln_mix_b=ln_mix_b,
               ln_ffn_g=ln_ffn_g, ln_ffn_b=ln_ffn_b)
    params = _prep_params(raw)
    bp, tp, d = x_prompt.shape
    bs, ts, _ = x_sample.shape
    n_rwkv, n_conv = w_r.shape[0], c_w_in.shape[0]
    nh = d // HEAD
    z_wkv = jnp.zeros((n_rwkv, bp, nh, HEAD, HEAD), F32)
    z_shift = jnp.zeros((n_rwkv, bp, d), F32)
    z_conv = jnp.zeros((n_conv, bp, CONV_STATE, d), F32)
    y_p, wkv_p, shift_p, conv_p = _run_trunk(x_prompt, z_shift, z_wkv, z_conv, params, _config(bp, tp))
    y_s, wkv_s, shift_s, conv_s = _run_trunk(x_sample, state_shift, state_wkv, state_conv, params, _config(bs, ts))
    return (y_p, y_s, wkv_p, shift_p, conv_p, wkv_s, shift_s, conv_s)
```

```python
import functools
import math

import jax
import jax.numpy as jnp
from jax import lax
from jax.experimental import pallas as pl
from jax.experimental.pallas import tpu as pltpu

F32 = jnp.float32
BF16 = jnp.bfloat16

SUBLANE = 8
HEAD = 64
GROUP = 4
GW = HEAD * GROUP
CHUNK = 64
WKV_UNITS = 16
CONV_W = 31
CONV_STATE = CONV_W - 1
HALO = 32
LN_EPS = 1e-5
GN_EPS = 64e-5
VMEM_LIMIT = 56 * 1024 * 1024

NN = ((1,), (0,))
NT = ((1,), (1,))


def _cparams(sem):
    return pltpu.CompilerParams(dimension_semantics=sem, vmem_limit_bytes=VMEM_LIMIT)


def _dot(a, b, dims=NN):
    return lax.dot_general(a, b, (dims, ((), ())), preferred_element_type=F32)


def _split(x):
    hi = x.astype(BF16)
    lo = (x - hi.astype(F32)).astype(BF16)
    return hi, lo


def _layer_norm(x, g, b):
    mu = jnp.mean(x, axis=-1, keepdims=True)
    xc = x - mu
    var = jnp.mean(xc * xc, axis=-1, keepdims=True)
    return xc * lax.rsqrt(var + LN_EPS) * g + b


def _sigmoid(x):
    return 1.0 / (1.0 + jnp.exp(-x))


def _lane_head(shape, dim):
    return lax.broadcasted_iota(jnp.int32, shape, dim) // HEAD


def _block_diag(x):
    head = _lane_head(x.shape, 1)
    return jnp.concatenate([jnp.where(head == h, x, 0.0) for h in range(GROUP)], axis=0)


def _head_ones():
    return jnp.where(_lane_head((GW, GW), 0) == _lane_head((GW, GW), 1), 1.0, 0.0).astype(BF16)


def _head_sum(x, ones):
    hi, lo = _split(x)
    return _dot(hi, ones) + _dot(lo, ones)


def _mlp_kernel(x_ref, wu_ref, wd_ref, g_ref, b_ref, o_ref, acc_ref, xb_ref, *, alpha):
    j = pl.program_id(1)

    @pl.when(j == 0)
    def _():
        xb_ref[...] = x_ref[...].astype(BF16)
        acc_ref[...] = jnp.zeros_like(acc_ref)

    h = jnp.maximum(_dot(xb_ref[...], wu_ref[...]), 0.0)
    acc_ref[...] += _dot((h * h).astype(BF16), wd_ref[...])

    @pl.when(j == pl.num_programs(1) - 1)
    def _():
        o_ref[...] = _layer_norm(alpha * x_ref[...] + acc_ref[...], g_ref[...], b_ref[...])


def _mlp(x, w_up, w_down, layer, g, b, alpha, tm, tf):
    rows, d = x.shape
    dff = w_up.shape[2]
    return pl.pallas_call(
        functools.partial(_mlp_kernel, alpha=alpha),
        grid=(rows // tm, dff // tf),
        in_specs=[
            pl.BlockSpec((tm, d), lambda i, j: (i, 0)),
            pl.BlockSpec((None, d, tf), lambda i, j: (layer, 0, j)),
            pl.BlockSpec((None, tf, d), lambda i, j: (layer, j, 0)),
            pl.BlockSpec((1, d), lambda i, j: (0, 0)),
            pl.BlockSpec((1, d), lambda i, j: (0, 0)),
        ],
        out_specs=pl.BlockSpec((tm, d), lambda i, j: (i, 0)),
        out_shape=jax.ShapeDtypeStruct((rows, d), F32),
        scratch_shapes=[pltpu.VMEM((tm, d), F32), pltpu.VMEM((tm, d), BF16)],
        compiler_params=_cparams(("parallel", "arbitrary")),
        name="mlp",
    )(x, w_up, w_down, g, b)


def _rwkv_proj_kernel(x_ref, tail_ref, shift_ref, mu_ref, w1_ref, a1_ref, g1_ref,
                      wr_ref, wk_ref, wv_ref, w2_ref, a2_ref, g2_ref,
                      w0_ref, a0_ref, kk_ref, ka_ref, rk_ref,
                      r_o, lw_o, k_o, v_o, kap_o, b_o, g_o, bonus_o, last_o,
                      xm_ref, hw_ref, ha_ref, hg_ref):
    j = pl.program_id(2)

    @pl.when(j == 0)
    def _():
        x = x_ref[0]
        rolled = pltpu.roll(x, 1, axis=0)
        first = lax.broadcasted_iota(jnp.int32, x.shape, 0) == 0
        before = jnp.where(pl.program_id(1) == 0, shift_ref[0], tail_ref[0, SUBLANE - 1:SUBLANE, :])
        xx = jnp.where(first, before, rolled) - x
        last_o[0] = x[x.shape[0] - SUBLANE:, :]
        mu = mu_ref[...]
        xm_ref[0] = (x + xx * mu[0:1]).astype(BF16)
        xm_ref[1] = (x + xx * mu[2:3]).astype(BF16)
        xm_ref[2] = (x + xx * mu[3:4]).astype(BF16)
        xw = (x + xx * mu[1:2]).astype(BF16)
        xa = (x + xx * mu[4:5]).astype(BF16)
        xg = (x + xx * mu[5:6]).astype(BF16)
        hw_ref[...] = jnp.tanh(_dot(xw, w1_ref[...])).astype(BF16)
        ha_ref[...] = _dot(xa, a1_ref[...]).astype(BF16)
        hg_ref[...] = _sigmoid(_dot(xg, g1_ref[...])).astype(BF16)

    r = _dot(xm_ref[0], wr_ref[...])
    k = _dot(xm_ref[1], wk_ref[...])
    v = _dot(xm_ref[2], wv_ref[...])
    z = -(w0_ref[...] + _dot(hw_ref[...], w2_ref[...]))
    w_log = -(jnp.maximum(z, 0.0) + jnp.log(1.0 + jnp.exp(-jnp.abs(z)))) - 0.5
    a = _sigmoid(a0_ref[...] + _dot(ha_ref[...], a2_ref[...]))
    g = _dot(hg_ref[...], g2_ref[...])

    kk = k * kk_ref[...]
    ones = _head_ones()
    slabs = range(0, kk.shape[1], GW)
    sq = kk * kk
    ss = jnp.concatenate([_head_sum(sq[:, s:s + GW], ones) for s in slabs], axis=1)
    kap = kk * lax.rsqrt(jnp.maximum(ss, 1e-24))
    k2 = k * (1.0 + (a - 1.0) * ka_ref[...])
    rkk = r * k2 * rk_ref[...]
    bonus = jnp.concatenate([_head_sum(rkk[:, s:s + GW], ones) for s in slabs], axis=1) * v

    r_o[0] = r
    lw_o[0] = -jnp.exp(w_log)
    k_o[0] = k2
    v_o[0] = v
    kap_o[0] = kap
    b_o[0] = kap * a
    g_o[0] = g
    bonus_o[0] = bonus


def _rwkv_proj(x, shift, p, tm, tn):
    bsz, t, d = x.shape
    nt = t // tm
    tail_blocks = tm // SUBLANE
    lw, la, lg = p['w1'].shape[1], p['a1'].shape[1], p['g1'].shape[1]
    full = lambda shape: pl.BlockSpec(shape, lambda b, i, j: (0,) * len(shape))
    col = lambda rows: pl.BlockSpec((rows, tn), lambda b, i, j: (0, j))
    out_spec = pl.BlockSpec((1, tm, tn), lambda b, i, j: (b, i, j))
    out_sds = jax.ShapeDtypeStruct((bsz, t, d), F32)
    return pl.pallas_call(
        _rwkv_proj_kernel,
        grid=(bsz, nt, d // tn),
        in_specs=[
            pl.BlockSpec((1, tm, d), lambda b, i, j: (b, i, 0)),
            pl.BlockSpec((1, SUBLANE, d), lambda b, i, j: (b, jnp.maximum(i * tail_blocks - 1, 0), 0)),
            pl.BlockSpec((1, 1, d), lambda b, i, j: (b, 0, 0)),
            full((8, d)), full((d, lw)), full((d, la)), full((d, lg)),
            col(d), col(d), col(d), col(lw), col(la), col(lg),
            col(1), col(1), col(1), col(1), col(1),
        ],
        out_specs=[out_spec] * 8 + [pl.BlockSpec((1, SUBLANE, d), lambda b, i, j: (b, 0, 0))],
        out_shape=[out_sds] * 8 + [jax.ShapeDtypeStruct((bsz, SUBLANE, d), F32)],
        scratch_shapes=[pltpu.VMEM((3, tm, d), BF16), pltpu.VMEM((tm, lw), BF16),
                        pltpu.VMEM((tm, la), BF16), pltpu.VMEM((tm, lg), BF16)],
        compiler_params=_cparams(("parallel", "arbitrary", "arbitrary")),
        name="rwkv_proj",
    )(x, x, shift[:, None, :], p['mu'], p['w1'], p['a1'], p['g1'], p['w_r'], p['w_k'], p['w_v'],
      p['w2'], p['a2'], p['g2'], p['w0'], p['a0'], p['k_k'], p['k_a'], p['r_k'])


def _head_transpose(x):
    xt = x.T
    return jnp.concatenate([xt[h * HEAD:(h + 1) * HEAD, :] for h in range(GROUP)], axis=1)


def _b16(x):
    return x.astype(BF16)


def _each(f, *lists):
    return [f(*args) for args in zip(*lists)]


def _bd16(x):
    return _b16(_block_diag(x))


def _wkv_chunks(r, lw, k, v, kap, b):
    c = CHUNK
    row = lax.broadcasted_iota(jnp.int32, (c, GW), 0)
    pos = lax.broadcasted_iota(jnp.int32, (c, GW), 1) % HEAD
    tri = jnp.where(lax.broadcasted_iota(jnp.int32, (c, c), 1) <= lax.broadcasted_iota(jnp.int32, (c, c), 0),
                    1.0, 0.0).astype(BF16)
    strict = pos < row
    lower = pos <= row
    diag = pos == row

    def cumsum(x):
        hi = _b16(x)
        rem = x - hi.astype(F32)
        mid = _b16(rem)
        lo = _b16(rem - mid.astype(F32))
        return _dot(tri, hi) + (_dot(tri, mid) + _dot(tri, lo))

    cum = _each(cumsum, lw)
    end = _each(lambda s: s[c - 1:c, :], cum)
    e_neg = _each(lambda s: jnp.exp(-s), cum)
    e_end = _each(lambda e, s: jnp.exp(e - s), end, cum)
    rt = _each(lambda x, s: x * jnp.exp(s), r, cum)
    kq = _each(lambda x, s, l: x * jnp.exp(s - l), kap, cum, lw)
    bh = _each(jnp.multiply, b, e_neg)
    kh = _each(jnp.multiply, k, e_neg)
    bp = _each(jnp.multiply, b, e_end)
    kp = _each(jnp.multiply, k, e_end)

    lhs = _each(lambda x, y: _b16(jnp.concatenate([x, y], axis=0)), kq, rt)
    gb = _each(lambda x, y: _dot(x, _bd16(y), NT), lhs, bh)
    gk = _each(lambda x, y: _dot(x, _bd16(y), NT), lhs, kh)
    l_ub = _each(lambda g: jnp.where(strict, g[:c], 0.0), gb)
    a_uk = _each(lambda g: jnp.where(strict, g[:c], 0.0), gk)
    a_rb = _each(lambda g: jnp.where(lower, g[c:], 0.0), gb)
    a_rk = _each(lambda g: jnp.where(lower, g[c:], 0.0), gk)

    stack = lambda x, y: _b16(jnp.concatenate([x, y], axis=0))
    nm = _each(jnp.negative, l_ub)
    pw = _each(lambda x: _dot(_b16(x), _bd16(x)), l_ub)
    for _ in range(int(math.log2(c)) - 2):
        res = _each(lambda x, n: _dot(stack(x, n), _bd16(x)), pw, nm)
        nm = _each(lambda n, x, y: n + x + y[c:], nm, pw, res)
        pw = _each(lambda y: y[:c], res)
    nm = _each(lambda n, x: n + x + _dot(_b16(n), _bd16(x)), nm, pw)
    nm16 = _each(_b16, nm)

    bpx = _each(lambda x: _b16(_head_transpose(x)), bp)
    kpx = _each(lambda x: _b16(_head_transpose(x)), kp)
    wa = _each(lambda x, y, z, w: _dot(jnp.concatenate([stack(x, y), z], axis=0), _bd16(w)), a_uk, a_rk, kpx, v)
    p = _each(lambda x, n: x + _dot(n, _bd16(x)), kq, nm16)
    u0 = _each(lambda x, n: -(x[:c] + _dot(n, _bd16(x[:c]))), wa, nm16)

    lhs2 = _each(lambda a, x: jnp.concatenate([_b16(a), x], axis=0), a_rb, bpx)
    rp = _each(lambda a, y: _dot(a, _bd16(y)), lhs2, p)
    ru = _each(lambda a, y: _dot(a, _bd16(y)), lhs2, u0)
    qp = _each(lambda x, y: x - y[:c], rt, rp)
    y0 = _each(lambda y, x: y[:c] + x[c:2 * c], ru, wa)
    m_cat = _each(lambda e, y: jnp.where(diag, jnp.exp(e), 0.0) - y[c:], end, rp)
    d_cat = _each(lambda y, x: y[c:] + x[2 * c:], ru, wa)
    return qp, y0, m_cat, d_cat


def _wkv_pre_kernel(r_ref, lw_ref, k_ref, v_ref, kap_ref, b_ref, qp_o, y0_o, m_o, d_o, *, nchunk, ngroup):
    units = [(ci, slice(ci * CHUNK, (ci + 1) * CHUNK), slice(gi * GW, (gi + 1) * GW))
             for ci in range(nchunk) for gi in range(ngroup)]
    load = lambda ref: [ref[0, rs, ls] for _, rs, ls in units]
    qp, y0, m_cat, d_cat = _wkv_chunks(load(r_ref), load(lw_ref), load(k_ref), load(v_ref),
                                       load(kap_ref), load(b_ref))
    for n, (ci, rs, ls) in enumerate(units):
        qp_o[0, rs, ls] = qp[n]
        y0_o[0, rs, ls] = y0[n]
        m_o[0, ci, :, ls] = m_cat[n]
        d_o[0, ci, :, ls] = d_cat[n]


def _wkv_pre(r, lw, k, v, kap, b, tc, tg):
    bsz, t, d = r.shape
    nc = t // CHUNK
    seq = pl.BlockSpec((1, tc * CHUNK, tg * GW), lambda bi, ci, gi: (bi, ci, gi))
    mat = pl.BlockSpec((1, tc, HEAD, tg * GW), lambda bi, ci, gi: (bi, ci, 0, gi))
    seq_sds = jax.ShapeDtypeStruct((bsz, t, d), F32)
    mat_sds = jax.ShapeDtypeStruct((bsz, nc, HEAD, d), F32)
    return pl.pallas_call(
        functools.partial(_wkv_pre_kernel, nchunk=tc, ngroup=tg),
        grid=(bsz, nc // tc, d // (tg * GW)),
        in_specs=[seq] * 6,
        out_specs=[seq, seq, mat, mat],
        out_shape=[seq_sds, seq_sds, mat_sds, mat_sds],
        compiler_params=_cparams(("parallel", "parallel", "parallel")),
        name="wkv_pre",
    )(r, lw, k, v, kap, b)


def _wkv_tail_kernel(qp_ref, y0_ref, m_ref, d_ref, h0_ref, bonus_ref, g_ref, x_ref, wo_ref,
                     gng_ref, gnb_ref, lng_ref, lnb_ref, o_ref, h_o, h_ref, y_ref,
                     *, nb, nchunk, ngroup, alpha):
    s = pl.program_id(1)
    rows_b = nchunk * CHUNK

    @pl.when(s == 0)
    def _():
        h_ref[...] = h0_ref[...]
        y_ref[...] = jnp.zeros_like(y_ref)

    ones = _head_ones()
    y = y_ref[...]
    parts = []
    for ls in range(0, y.shape[1], GW):
        ys = y[:, ls:ls + GW]
        mean = _dot(_b16(ys), ones) * (1.0 / HEAD)
        yc = ys - mean
        var = _dot(_b16(yc * yc), ones) * (1.0 / HEAD)
        parts.append(yc * lax.rsqrt(var + GN_EPS))
    yn = jnp.concatenate(parts, axis=1) * gng_ref[...] + gnb_ref[...]
    rows_of = lambda ref: jnp.concatenate([ref[bi] for bi in range(nb)], axis=0)
    z = _b16((yn + rows_of(bonus_ref)) * rows_of(g_ref))
    res = _layer_norm(alpha * rows_of(x_ref) + _dot(z, wo_ref[...]), lng_ref[...], lnb_ref[...])
    for bi in range(nb):
        o_ref[bi] = res[bi * rows_b:(bi + 1) * rows_b]

    for c in range(nchunk):
        rows = slice(c * CHUNK, (c + 1) * CHUNK)
        for bi in range(nb):
            for gi in range(ngroup):
                lanes = slice(gi * GW, (gi + 1) * GW)
                h_hi, h_lo = _split(_block_diag(h_ref[bi, :, lanes]))
                m_hi, m_lo = _split(m_ref[bi, c, :, lanes])
                lhs = jnp.concatenate([_b16(qp_ref[bi, rows, lanes]), m_hi, m_lo], axis=0)
                out = _dot(lhs, h_hi)
                y_ref[bi * rows_b + c * CHUNK:bi * rows_b + (c + 1) * CHUNK, lanes] = (
                    out[:CHUNK] + y0_ref[bi, rows, lanes])
                h_ref[bi, :, lanes] = (out[CHUNK:2 * CHUNK] + (out[2 * CHUNK:] + _dot(m_hi, h_lo))
                                       + d_ref[bi, c, :, lanes])

    @pl.when(s == pl.num_programs(1) - 2)
    def _():
        h_o[...] = h_ref[...]


def _wkv_tail(qp, y0, m, dmat, h0, bonus, g, x, w_o, gn_g, gn_b, ln_g, ln_b, alpha, nb, tc):
    bsz, t, d = qp.shape
    steps = t // (tc * CHUNK)
    cur = lambda bi, s: (bi, jnp.minimum(s, steps - 1), 0)
    prv = lambda bi, s: (bi, jnp.maximum(s - 1, 0), 0)
    seq = pl.BlockSpec((nb, tc * CHUNK, d), cur)
    mat = pl.BlockSpec((nb, tc, HEAD, d), lambda bi, s: (bi, jnp.minimum(s, steps - 1), 0, 0))
    st = pl.BlockSpec((nb, HEAD, d), lambda bi, s: (bi, 0, 0))
    old = pl.BlockSpec((nb, tc * CHUNK, d), prv)
    vec = pl.BlockSpec((1, d), lambda bi, s: (0, 0))
    return pl.pallas_call(
        functools.partial(_wkv_tail_kernel, nb=nb, nchunk=tc, ngroup=d // GW, alpha=alpha),
        grid=(bsz // nb, steps + 1),
        in_specs=[seq, seq, mat, mat, st, old, old, old,
                  pl.BlockSpec((d, d), lambda bi, s: (0, 0), pipeline_mode=pl.Buffered(1)),
                  vec, vec, vec, vec],
        out_specs=[old, st],
        out_shape=[jax.ShapeDtypeStruct((bsz, t, d), F32), jax.ShapeDtypeStruct((bsz, HEAD, d), F32)],
        scratch_shapes=[pltpu.VMEM((nb, HEAD, d), F32), pltpu.VMEM((nb * tc * CHUNK, d), F32)],
        compiler_params=_cparams(("parallel", "arbitrary")),
        name="wkv_tail",
    )(qp, y0, m, dmat, h0, bonus, g, x, w_o, gn_g, gn_b, ln_g, ln_b)


def _conv_kernel(xc_ref, xp_ref, st_ref, win_w, bin_ref, dw_ref, dwb_ref, cg_ref, cb_ref, wo_ref, bo_ref,
                 lng_ref, lnb_ref, o_ref, st_o, win_ref, u_ref, c_ref, *, alpha, tm, tn, skew):
    step = pl.program_id(1)
    d = xc_ref.shape[2]

    @pl.when(step == 0)
    def _():
        win_ref[...] = jnp.zeros_like(win_ref)

    def conv_half():
        zrows = tm + 8
        for ls in range(0, d, GW):
            lanes = slice(ls, ls + GW)
            acc = jnp.zeros((tm, GW), F32)
            for r in range(8):
                z = None
                for q in range((HALO + 8) // 8):
                    j = 8 * q + r - 2
                    if j < 0 or j >= CONV_W:
                        continue
                    nrows = tm if r == 0 else zrows
                    term = win_ref[8 * q:8 * q + nrows, lanes] * dw_ref[j:j + 1, lanes]
                    z = term if z is None else z + term
                if r == 0:
                    acc = acc + z
                else:
                    acc = acc + pltpu.roll(z, zrows - r, axis=0)[:tm]
            c_ref[:, lanes] = acc

        c = _layer_norm(c_ref[...] + dwb_ref[...], cg_ref[...], cb_ref[...])
        c = (c * _sigmoid(c)).astype(BF16)
        out = _dot(c, wo_ref[...]) + bo_ref[...]
        o_ref[0] = _layer_norm(alpha * xp_ref[0] + out, lng_ref[...], lnb_ref[...])

    def glu_half():
        xb = _b16(xc_ref[0])
        for cs in range(0, d, tn):
            val = _dot(xb, win_w[:, cs:cs + tn]) + bin_ref[:, cs:cs + tn]
            gate = _dot(xb, win_w[:, d + cs:d + cs + tn]) + bin_ref[:, d + cs:d + cs + tn]
            u_ref[:, cs:cs + tn] = val * _sigmoid(gate)
        win_ref[0:HALO, :] = jnp.where(step == 0, st_ref[0], win_ref[tm:tm + HALO, :])
        win_ref[HALO:HALO + tm, :] = u_ref[...]

    if skew:
        conv_half()
        glu_half()
    else:
        glu_half()
        conv_half()

    @pl.when(step == pl.num_programs(1) - (2 if skew else 1))
    def _():
        st_o[0] = u_ref[tm - HALO:tm, :]


def _conv(x, state_pad, w_in, b_in, dw, dw_b, cln_g, cln_b, w_out, b_out, ln_g, ln_b, alpha, tm, tn):
    bsz, t, d = x.shape
    steps = t // tm
    skew = steps > 1
    vec = pl.BlockSpec((1, d), lambda b, s: (0, 0))
    once = lambda shape: pl.BlockSpec(shape, lambda b, s: (0, 0), pipeline_mode=pl.Buffered(1))
    hist = pl.BlockSpec((1, HALO, d), lambda b, s: (b, 0, 0))
    if skew:
        glu_tile = pl.BlockSpec((1, tm, d), lambda b, s: (b, jnp.minimum(s, steps - 1), 0))
        prev_tile = pl.BlockSpec((1, tm, d), lambda b, s: (b, jnp.maximum(s - 1, 0), 0))
    else:
        glu_tile = prev_tile = pl.BlockSpec((1, tm, d), lambda b, s: (b, s, 0))
    return pl.pallas_call(
        functools.partial(_conv_kernel, alpha=alpha, tm=tm, tn=tn, skew=skew),
        grid=(bsz, steps + 1 if skew else steps),
        in_specs=[
            glu_tile,
            prev_tile, hist,
            once((d, 2 * d)), pl.BlockSpec((1, 2 * d), lambda b, s: (0, 0)),
            pl.BlockSpec((HALO, d), lambda b, s: (0, 0)),
            vec, vec, vec,
            once((d, d)),
            vec, vec, vec,
        ],
        out_specs=[prev_tile, hist],
        out_shape=[jax.ShapeDtypeStruct((bsz, t, d), F32), jax.ShapeDtypeStruct((bsz, HALO, d), F32)],
        scratch_shapes=[pltpu.VMEM((HALO + tm, d), F32), pltpu.VMEM((tm, d), F32), pltpu.VMEM((tm, d), F32)],
        compiler_params=_cparams(("parallel", "arbitrary")),
        name="conv",
    )(x, x, state_pad, w_in, b_in, dw, dw_b, cln_g, cln_b, w_out, b_out, ln_g, ln_b)


def _pad_cols(w, n):
    return jnp.pad(w, ((0, 0), (0, n - w.shape[1])))


def _pad_rows(w, n):
    return jnp.pad(w, ((0, n - w.shape[0]), (0, 0)))


def _round_up(n, m):
    return -(-n // m) * m


def _prep_params(raw):
    depth = raw['w_up'].shape[0]
    n_rwkv = raw['w_r'].shape[0]
    n_conv = raw['c_w_in'].shape[0]
    row = lambda v: v.reshape(1, -1)
    rwkv = []
    for j in range(n_rwkv):
        lw = _round_up(raw['w1'].shape[2], 128)
        la = _round_up(raw['a1'].shape[2], 128)
        rwkv.append(dict(
            mu=_pad_rows(raw['mu'][j], 8),
            w_r=raw['w_r'][j].astype(BF16), w_k=raw['w_k'][j].astype(BF16), w_v=raw['w_v'][j].astype(BF16),
            w_o=raw['w_o'][j].astype(BF16),
            w1=_pad_cols(raw['w1'][j], lw).astype(BF16), w2=_pad_rows(raw['w2'][j], lw).astype(BF16),
            a1=_pad_cols(raw['a1'][j], la).astype(BF16), a2=_pad_rows(raw['a2'][j], la).astype(BF16),
            g1=raw['g1'][j].astype(BF16), g2=raw['g2'][j].astype(BF16),
            w0=row(raw['w0'][j]), a0=row(raw['a0'][j]), k_k=row(raw['k_k'][j]), k_a=row(raw['k_a'][j]),
            r_k=row(raw['r_k'][j]), gn_g=row(raw['gn_g'][j]), gn_b=row(raw['gn_b'][j])))
    conv = []
    for j in range(n_conv):
        conv.append(dict(
            w_in=raw['c_w_in'][j].astype(BF16), b_in=row(raw['c_b_in'][j]),
            dw=_pad_rows(raw['c_dw'][j], HALO), dw_b=row(raw['c_dw_b'][j]),
            ln_g=row(raw['c_ln_g'][j]), ln_b=row(raw['c_ln_b'][j]),
            w_out=raw['c_w_out'][j].astype(BF16), b_out=row(raw['c_b_out'][j])))
    w_up, w_down = raw['w_up'].astype(BF16), raw['w_down'].astype(BF16)
    mlp = [dict(w_up=w_up, w_down=w_down,
                mix_g=row(raw['ln_mix_g'][i]), mix_b=row(raw['ln_mix_b'][i]),
                ffn_g=row(raw['ln_ffn_g'][i]), ffn_b=row(raw['ln_ffn_b'][i])) for i in range(depth)]
    return rwkv, conv, mlp


def _rwkv_layer(x, shift, wkv, p, lnp, alpha, cfg):
    bsz, t, d = x.shape
    r, lw, k, v, kap, b, g, bonus, last = _rwkv_proj(x, shift, p, cfg['tm_seq'], cfg['tn'])
    tp = _round_up(t, CHUNK)
    xp = x
    if tp != t:
        pad = lambda a: jnp.pad(a, ((0, 0), (0, tp - t), (0, 0)))
        r, lw, k, v, kap, b, g, bonus, xp = [pad(a) for a in (r, lw, k, v, kap, b, g, bonus, x)]
    qp, y0, m, dmat = _wkv_pre(r, lw, k, v, kap, b, cfg['tc_pre'], cfg['tg_pre'])
    nh = d // HEAD
    h0 = jnp.transpose(wkv, (0, 3, 1, 2)).reshape(bsz, HEAD, d)
    x1, h1 = _wkv_tail(qp, y0, m, dmat, h0, bonus, g, xp, p['w_o'], p['gn_g'], p['gn_b'],
                       lnp['mix_g'], lnp['mix_b'], alpha, cfg['nb_seq'], cfg['tc_seq'])
    new_wkv = jnp.transpose(h1.reshape(bsz, HEAD, nh, HEAD), (0, 2, 3, 1))
    return x1[:, :t], last[:, SUBLANE - 1], new_wkv


def _conv_layer(x, state, p, lnp, alpha, cfg):
    bsz, t, d = x.shape
    assert t >= HALO
    state_pad = jnp.pad(state, ((0, 0), (HALO - CONV_STATE, 0), (0, 0)))
    x1, hist = _conv(x, state_pad, p['w_in'], p['b_in'], p['dw'], p['dw_b'], p['ln_g'], p['ln_b'],
                     p['w_out'], p['b_out'], lnp['mix_g'], lnp['mix_b'], alpha, cfg['tm_conv'], cfg['tn'])
    return x1, hist[:, HALO - CONV_STATE:]


def _run_trunk(x, st_shift, st_wkv, st_conv, params, cfg):
    rwkv, conv, mlp = params
    depth = len(mlp)
    alpha = (2.0 * depth) ** 0.25
    bsz, t, d = x.shape
    new_shift, new_wkv, new_conv = [], [], []
    for i in range(depth):
        j = i // 2
        if i % 2 == 0:
            x, sh, s = _rwkv_layer(x, st_shift[j], st_wkv[j], rwkv[j], mlp[i], alpha, cfg)
            new_shift.append(sh)
            new_wkv.append(s)
        else:
            x, cs = _conv_layer(x, st_conv[j], conv[j], mlp[i], alpha, cfg)
            new_conv.append(cs)
        x = _mlp(x.reshape(bsz * t, d), mlp[i]['w_up'], mlp[i]['w_down'], i, mlp[i]['ffn_g'], mlp[i]['ffn_b'],
                 alpha, cfg['tm_rows'], cfg['tf']).reshape(bsz, t, d)
    return x, jnp.stack(new_wkv), jnp.stack(new_shift), jnp.stack(new_conv)


def _config(bsz, t, d):
    rows = bsz * t
    tm_seq = min(t, 512)
    nchunk = _round_up(t, CHUNK) // CHUNK
    tc_pre = min(nchunk, WKV_UNITS)
    tg_pre = min(WKV_UNITS // tc_pre, d // GW)
    return dict(
        tm_seq=tm_seq, tn=512, tm_rows=min(rows, 512), tf=1024,
        tm_conv=min(t, 256), tc_pre=tc_pre, tg_pre=tg_pre, tc_seq=min(nchunk, 2), nb_seq=min(bsz, 2))


def kernel(x_prompt, x_sample, state_wkv, state_shift, state_conv, mu, w_r, w_k, w_v, w_o, w0, w1, w2, a0, a1, a2, g1, g2, k_k, k_a, r_k, gn_g, gn_b, c_w_in, c_b_in, c_dw, c_dw_b, c_ln_g, c_ln_b, c_w_out, c_b_out, w_up, w_down, ln_mix_g, ln_mix_b, ln_ffn_g, ln_ffn_b):
    raw = dict(mu=mu, w_r=w_r, w_k=w_k, w_v=w_v, w_o=w_o, w0=w0, w1=w1, w2=w2, a0=a0, a1=a1, a2=a2,
               g1=g1, g2=g2, k_k=k_k, k_a=k_a, r_k=r_k, gn_g=gn_g, gn_b=gn_b, c_w_in=c_w_in,
               c_b_in=c_b_in, c_dw=c_dw, c_dw_b=c_dw_b, c_ln_g=c_ln_g, c_ln_b=c_ln_b, c_w_out=c_w_out,
               c_b_out=c_b_out, w_up=w_up, w_down=w_down, ln_mix_g=ln_mix_g, ln_mix_b=ln_mix_b,
               ln_ffn_g=ln_ffn_g, ln_ffn_b=ln_ffn_b)
    params = _prep_params(raw)
    bp, tp, d = x_prompt.shape
    bs, ts, _ = x_sample.shape
    n_rwkv, n_conv = w_r.shape[0], c_w_in.shape[0]
    nh = d // HEAD
    z_wkv = jnp.zeros((n_rwkv, bp, nh, HEAD, HEAD), F32)
    z_shift = jnp.zeros((n_rwkv, bp, d), F32)
    z_conv = jnp.zeros((n_conv, bp, CONV_STATE, d), F32)
    y_p, wkv_p, shift_p, conv_p = _run_trunk(x_prompt, z_shift, z_wkv, z_conv, params, _config(bp, tp, d))
    y_s, wkv_s, shift_s, conv_s = _run_trunk(x_sample, state_shift, state_wkv, state_conv, params,
                                             _config(bs, ts, d))
    return (y_p, y_s, wkv_p, shift_p, conv_p, wkv_s, shift_s, conv_s)
```

```python
import functools
import math

import jax
import jax.numpy as jnp
from jax import lax
from jax.experimental import pallas as pl
from jax.experimental.pallas import tpu as pltpu

F32 = jnp.float32
BF16 = jnp.bfloat16

SUBLANE = 8
HEAD = 64
GROUP = 4
GW = HEAD * GROUP
CHUNK = 64
WKV_UNITS = 16
CONV_W = 31
CONV_STATE = CONV_W - 1
HALO = 32
LN_EPS = 1e-5
GN_EPS = 64e-5
VMEM_LIMIT = 56 * 1024 * 1024

NN = ((1,), (0,))
NT = ((1,), (1,))


def _cparams(sem):
    return pltpu.CompilerParams(dimension_semantics=sem, vmem_limit_bytes=VMEM_LIMIT)


def _dot(a, b, dims=NN):
    return lax.dot_general(a, b, (dims, ((), ())), preferred_element_type=F32)


def _split(x):
    hi = x.astype(BF16)
    lo = (x - hi.astype(F32)).astype(BF16)
    return hi, lo


def _layer_norm(x, g, b):
    mu = jnp.mean(x, axis=-1, keepdims=True)
    xc = x - mu
    var = jnp.mean(xc * xc, axis=-1, keepdims=True)
    return xc * lax.rsqrt(var + LN_EPS) * g + b


def _sigmoid(x):
    return 1.0 / (1.0 + jnp.exp(-x))


def _lane_head(shape, dim):
    return lax.broadcasted_iota(jnp.int32, shape, dim) // HEAD


def _block_diag(x):
    head = _lane_head(x.shape, 1)
    return jnp.concatenate([jnp.where(head == h, x, 0.0) for h in range(GROUP)], axis=0)


def _head_ones():
    return jnp.where(_lane_head((GW, GW), 0) == _lane_head((GW, GW), 1), 1.0, 0.0).astype(BF16)


def _head_sum(x, ones):
    hi, lo = _split(x)
    return _dot(hi, ones) + _dot(lo, ones)


def _mlp_kernel(x_ref, wu_ref, wd_ref, g_ref, b_ref, o_ref, acc_ref, xb_ref, *, alpha):
    j = pl.program_id(1)

    @pl.when(j == 0)
    def _():
        xb_ref[...] = x_ref[...].astype(BF16)
        acc_ref[...] = jnp.zeros_like(acc_ref)

    h = jnp.maximum(_dot(xb_ref[...], wu_ref[...]), 0.0)
    acc_ref[...] += _dot((h * h).astype(BF16), wd_ref[...])

    @pl.when(j == pl.num_programs(1) - 1)
    def _():
        o_ref[...] = _layer_norm(alpha * x_ref[...] + acc_ref[...], g_ref[...], b_ref[...])


def _mlp(x, w_up, w_down, layer, g, b, alpha, tm, tf):
    rows, d = x.shape
    dff = w_up.shape[2]
    return pl.pallas_call(
        functools.partial(_mlp_kernel, alpha=alpha),
        grid=(rows // tm, dff // tf),
        in_specs=[
            pl.BlockSpec((tm, d), lambda i, j: (i, 0)),
            pl.BlockSpec((None, d, tf), lambda i, j: (layer, 0, j)),
            pl.BlockSpec((None, tf, d), lambda i, j: (layer, j, 0)),
            pl.BlockSpec((1, d), lambda i, j: (0, 0)),
            pl.BlockSpec((1, d), lambda i, j: (0, 0)),
        ],
        out_specs=pl.BlockSpec((tm, d), lambda i, j: (i, 0)),
        out_shape=jax.ShapeDtypeStruct((rows, d), F32),
        scratch_shapes=[pltpu.VMEM((tm, d), F32), pltpu.VMEM((tm, d), BF16)],
        compiler_params=_cparams(("parallel", "arbitrary")),
        name="mlp",
    )(x, w_up, w_down, g, b)


def _rwkv_proj_kernel(x_ref, tail_ref, shift_ref, mu_ref, w1_ref, a1_ref, g1_ref,
                      wr_ref, wk_ref, wv_ref, w2_ref, a2_ref, g2_ref,
                      w0_ref, a0_ref, kk_ref, ka_ref, rk_ref,
                      r_o, lw_o, k_o, v_o, kap_o, b_o, g_o, bonus_o, last_o,
                      xm_ref, hw_ref, ha_ref, hg_ref, *, seq_rows):
    j = pl.program_id(2)

    @pl.when(j == 0)
    def _():
        x = x_ref[0]
        rolled = pltpu.roll(x, 1, axis=0)
        row = lax.broadcasted_iota(jnp.int32, x.shape, 0)
        if seq_rows is None:
            first = row == 0
            before = jnp.where(pl.program_id(1) == 0, shift_ref[0], tail_ref[0, SUBLANE - 1:SUBLANE, :])
        else:
            first = row % seq_rows == 0
            before = shift_ref[0]
        xx = jnp.where(first, before, rolled) - x
        last_o[0] = x[x.shape[0] - SUBLANE:, :]
        mu = mu_ref[...]
        xm_ref[0] = (x + xx * mu[0:1]).astype(BF16)
        xm_ref[1] = (x + xx * mu[2:3]).astype(BF16)
        xm_ref[2] = (x + xx * mu[3:4]).astype(BF16)
        xw = (x + xx * mu[1:2]).astype(BF16)
        xa = (x + xx * mu[4:5]).astype(BF16)
        xg = (x + xx * mu[5:6]).astype(BF16)
        hw_ref[...] = jnp.tanh(_dot(xw, w1_ref[...])).astype(BF16)
        ha_ref[...] = _dot(xa, a1_ref[...]).astype(BF16)
        hg_ref[...] = _sigmoid(_dot(xg, g1_ref[...])).astype(BF16)

    r = _dot(xm_ref[0], wr_ref[...])
    k = _dot(xm_ref[1], wk_ref[...])
    v = _dot(xm_ref[2], wv_ref[...])
    z = -(w0_ref[...] + _dot(hw_ref[...], w2_ref[...]))
    w_log = -(jnp.maximum(z, 0.0) + jnp.log(1.0 + jnp.exp(-jnp.abs(z)))) - 0.5
    a = _sigmoid(a0_ref[...] + _dot(ha_ref[...], a2_ref[...]))
    g = _dot(hg_ref[...], g2_ref[...])

    kk = k * kk_ref[...]
    ones = _head_ones()
    slabs = range(0, kk.shape[1], GW)
    sq = kk * kk
    ss = jnp.concatenate([_head_sum(sq[:, s:s + GW], ones) for s in slabs], axis=1)
    kap = kk * lax.rsqrt(jnp.maximum(ss, 1e-24))
    k2 = k * (1.0 + (a - 1.0) * ka_ref[...])
    rkk = r * k2 * rk_ref[...]
    bonus = jnp.concatenate([_dot(_b16(rkk[:, s:s + GW]), ones) for s in slabs], axis=1) * v

    r_o[0] = r
    lw_o[0] = -jnp.exp(w_log)
    k_o[0] = k2
    v_o[0] = v
    kap_o[0] = kap
    b_o[0] = kap * a
    g_o[0] = g
    bonus_o[0] = bonus


def _rwkv_proj(x, shift, p, tm, tn, seq_rows=None):
    bsz, t, d = x.shape
    nt = t // tm
    tail_blocks = tm // SUBLANE
    if seq_rows is None:
        shift_arr, shift_spec = shift[:, None, :], pl.BlockSpec((1, 1, d), lambda b, i, j: (b, 0, 0))
    else:
        shift_arr, shift_spec = shift, pl.BlockSpec((1, tm, d), lambda b, i, j: (b, i, 0))
    lw, la, lg = p['w1'].shape[1], p['a1'].shape[1], p['g1'].shape[1]
    full = lambda shape: pl.BlockSpec(shape, lambda b, i, j: (0,) * len(shape))
    col = lambda rows: pl.BlockSpec((rows, tn), lambda b, i, j: (0, j))
    out_spec = pl.BlockSpec((1, tm, tn), lambda b, i, j: (b, i, j))
    out_sds = jax.ShapeDtypeStruct((bsz, t, d), F32)
    return pl.pallas_call(
        functools.partial(_rwkv_proj_kernel, seq_rows=seq_rows),
        grid=(bsz, nt, d // tn),
        in_specs=[
            pl.BlockSpec((1, tm, d), lambda b, i, j: (b, i, 0)),
            pl.BlockSpec((1, SUBLANE, d), lambda b, i, j: (b, jnp.maximum(i * tail_blocks - 1, 0), 0)),
            shift_spec,
            full((8, d)), full((d, lw)), full((d, la)), full((d, lg)),
            col(d), col(d), col(d), col(lw), col(la), col(lg),
            col(1), col(1), col(1), col(1), col(1),
        ],
        out_specs=[out_spec] * 8 + [pl.BlockSpec((1, SUBLANE, d), lambda b, i, j: (b, 0, 0))],
        out_shape=[out_sds] * 8 + [jax.ShapeDtypeStruct((bsz, SUBLANE, d), F32)],
        scratch_shapes=[pltpu.VMEM((3, tm, d), BF16), pltpu.VMEM((tm, lw), BF16),
                        pltpu.VMEM((tm, la), BF16), pltpu.VMEM((tm, lg), BF16)],
        compiler_params=_cparams(("parallel", "arbitrary", "arbitrary")),
        name="rwkv_proj",
    )(x, x, shift_arr, p['mu'], p['w1'], p['a1'], p['g1'], p['w_r'], p['w_k'], p['w_v'],
      p['w2'], p['a2'], p['g2'], p['w0'], p['a0'], p['k_k'], p['k_a'], p['r_k'])


def _head_transpose(x):
    xt = x.T
    return jnp.concatenate([xt[h * HEAD:(h + 1) * HEAD, :] for h in range(GROUP)], axis=1)


def _b16(x):
    return x.astype(BF16)


def _each(f, *lists):
    return [f(*args) for args in zip(*lists)]


def _bd16(x):
    return _b16(_block_diag(x))


def _wkv_chunks(r, lw, k, v, kap, b):
    c = CHUNK
    row = lax.broadcasted_iota(jnp.int32, (c, GW), 0)
    pos = lax.broadcasted_iota(jnp.int32, (c, GW), 1) % HEAD
    tri = jnp.where(lax.broadcasted_iota(jnp.int32, (c, c), 1) <= lax.broadcasted_iota(jnp.int32, (c, c), 0),
                    1.0, 0.0).astype(BF16)
    strict = pos < row
    lower = pos <= row
    diag = pos == row

    def cumsum(x):
        hi = _b16(x)
        rem = x - hi.astype(F32)
        mid = _b16(rem)
        lo = _b16(rem - mid.astype(F32))
        return _dot(tri, hi) + (_dot(tri, mid) + _dot(tri, lo))

    cum = _each(cumsum, lw)
    end = _each(lambda s: s[c - 1:c, :], cum)
    e_neg = _each(lambda s: jnp.exp(-s), cum)
    e_end = _each(lambda e, s: jnp.exp(e - s), end, cum)
    rt = _each(lambda x, s: x * jnp.exp(s), r, cum)
    kq = _each(lambda x, s, l: x * jnp.exp(s - l), kap, cum, lw)
    bh = _each(jnp.multiply, b, e_neg)
    kh = _each(jnp.multiply, k, e_neg)
    bp = _each(jnp.multiply, b, e_end)
    kp = _each(jnp.multiply, k, e_end)

    lhs = _each(lambda x, y: _b16(jnp.concatenate([x, y], axis=0)), kq, rt)
    gb = _each(lambda x, y: _dot(x, _bd16(y), NT), lhs, bh)
    gk = _each(lambda x, y: _dot(x, _bd16(y), NT), lhs, kh)
    l_ub = _each(lambda g: jnp.where(strict, g[:c], 0.0), gb)
    a_uk = _each(lambda g: jnp.where(strict, g[:c], 0.0), gk)
    a_rb = _each(lambda g: jnp.where(lower, g[c:], 0.0), gb)
    a_rk = _each(lambda g: jnp.where(lower, g[c:], 0.0), gk)

    stack = lambda x, y: _b16(jnp.concatenate([x, y], axis=0))
    nm = _each(jnp.negative, l_ub)
    pw = _each(lambda x: _dot(_b16(x), _bd16(x)), l_ub)
    for _ in range(int(math.log2(c)) - 2):
        res = _each(lambda x, n: _dot(stack(x, n), _bd16(x)), pw, nm)
        nm = _each(lambda n, x, y: n + x + y[c:], nm, pw, res)
        pw = _each(lambda y: y[:c], res)
    nm = _each(lambda n, x: n + x + _dot(_b16(n), _bd16(x)), nm, pw)
    nm16 = _each(_b16, nm)

    bpx = _each(lambda x: _b16(_head_transpose(x)), bp)
    kpx = _each(lambda x: _b16(_head_transpose(x)), kp)
    wa = _each(lambda x, y, z, w: _dot(jnp.concatenate([stack(x, y), z], axis=0), _bd16(w)), a_uk, a_rk, kpx, v)
    p = _each(lambda x, n: x + _dot(n, _bd16(x)), kq, nm16)
    u0 = _each(lambda x, n: -(x[:c] + _dot(n, _bd16(x[:c]))), wa, nm16)

    lhs2 = _each(lambda a, x: jnp.concatenate([_b16(a), x], axis=0), a_rb, bpx)
    rp = _each(lambda a, y: _dot(a, _bd16(y)), lhs2, p)
    ru = _each(lambda a, y: _dot(a, _bd16(y)), lhs2, u0)
    qp = _each(lambda x, y: x - y[:c], rt, rp)
    y0 = _each(lambda y, x: y[:c] + x[c:2 * c], ru, wa)
    m_cat = _each(lambda e, y: jnp.where(diag, jnp.exp(e), 0.0) - y[c:], end, rp)
    d_cat = _each(lambda y, x: y[c:] + x[2 * c:], ru, wa)
    return qp, y0, m_cat, d_cat


def _wkv_pre_kernel(r_ref, lw_ref, k_ref, v_ref, kap_ref, b_ref, qp_o, y0_o, m_o, d_o, *, nchunk, ngroup):
    units = [(ci, slice(ci * CHUNK, (ci + 1) * CHUNK), slice(gi * GW, (gi + 1) * GW))
             for ci in range(nchunk) for gi in range(ngroup)]
    load = lambda ref: [ref[0, rs, ls] for _, rs, ls in units]
    qp, y0, m_cat, d_cat = _wkv_chunks(load(r_ref), load(lw_ref), load(k_ref), load(v_ref),
                                       load(kap_ref), load(b_ref))
    for n, (ci, rs, ls) in enumerate(units):
        qp_o[0, rs, ls] = qp[n]
        y0_o[0, rs, ls] = y0[n]
        m_o[0, ci, :, ls] = m_cat[n]
        d_o[0, ci, :, ls] = d_cat[n]


def _wkv_pre(r, lw, k, v, kap, b, tc, tg):
    bsz, t, d = r.shape
    nc = t // CHUNK
    seq = pl.BlockSpec((1, tc * CHUNK, tg * GW), lambda bi, ci, gi: (bi, ci, gi))
    mat = pl.BlockSpec((1, tc, HEAD, tg * GW), lambda bi, ci, gi: (bi, ci, 0, gi))
    seq_sds = jax.ShapeDtypeStruct((bsz, t, d), F32)
    mat_sds = jax.ShapeDtypeStruct((bsz, nc, HEAD, d), F32)
    return pl.pallas_call(
        functools.partial(_wkv_pre_kernel, nchunk=tc, ngroup=tg),
        grid=(bsz, nc // tc, d // (tg * GW)),
        in_specs=[seq] * 6,
        out_specs=[seq, seq, mat, mat],
        out_shape=[seq_sds, seq_sds, mat_sds, mat_sds],
        compiler_params=_cparams(("parallel", "parallel", "parallel")),
        name="wkv_pre",
    )(r, lw, k, v, kap, b)


def _wkv_tail_kernel(qp_ref, y0_ref, m_ref, d_ref, h0_ref, bonus_ref, g_ref, x_ref, wo_ref,
                     gng_ref, gnb_ref, lng_ref, lnb_ref, o_ref, h_o, h_ref, y_ref,
                     *, nb, nchunk, ngroup, alpha):
    s = pl.program_id(1)
    rows_b = nchunk * CHUNK

    @pl.when(s == 0)
    def _():
        h_ref[...] = h0_ref[...]
        y_ref[...] = jnp.zeros_like(y_ref)

    ones = _head_ones()
    y = y_ref[...]
    parts = []
    for ls in range(0, y.shape[1], GW):
        ys = y[:, ls:ls + GW]
        mean = _dot(_b16(ys), ones) * (1.0 / HEAD)
        yc = ys - mean
        var = _dot(_b16(yc * yc), ones) * (1.0 / HEAD)
        parts.append(yc * lax.rsqrt(var + GN_EPS))
    yn = jnp.concatenate(parts, axis=1) * gng_ref[...] + gnb_ref[...]
    rows_of = lambda ref: jnp.concatenate([ref[bi] for bi in range(nb)], axis=0)
    z = _b16((yn + rows_of(bonus_ref)) * rows_of(g_ref))
    res = _layer_norm(alpha * rows_of(x_ref) + _dot(z, wo_ref[...]), lng_ref[...], lnb_ref[...])
    for bi in range(nb):
        o_ref[bi] = res[bi * rows_b:(bi + 1) * rows_b]

    for c in range(nchunk):
        rows = slice(c * CHUNK, (c + 1) * CHUNK)
        for bi in range(nb):
            for gi in range(ngroup):
                lanes = slice(gi * GW, (gi + 1) * GW)
                h_hi, h_lo = _split(_block_diag(h_ref[bi, :, lanes]))
                m_hi, m_lo = _split(m_ref[bi, c, :, lanes])
                lhs = jnp.concatenate([_b16(qp_ref[bi, rows, lanes]), m_hi, m_lo], axis=0)
                out = _dot(lhs, h_hi)
                y_ref[bi * rows_b + c * CHUNK:bi * rows_b + (c + 1) * CHUNK, lanes] = (
                    out[:CHUNK] + y0_ref[bi, rows, lanes])
                h_ref[bi, :, lanes] = (out[CHUNK:2 * CHUNK] + (out[2 * CHUNK:] + _dot(m_hi, h_lo))
                                       + d_ref[bi, c, :, lanes])

    @pl.when(s == pl.num_programs(1) - 2)
    def _():
        h_o[...] = h_ref[...]


def _wkv_tail(qp, y0, m, dmat, h0, bonus, g, x, w_o, gn_g, gn_b, ln_g, ln_b, alpha, nb, tc):
    bsz, t, d = qp.shape
    steps = t // (tc * CHUNK)
    cur = lambda bi, s: (bi, jnp.minimum(s, steps - 1), 0)
    prv = lambda bi, s: (bi, jnp.maximum(s - 1, 0), 0)
    seq = pl.BlockSpec((nb, tc * CHUNK, d), cur)
    mat = pl.BlockSpec((nb, tc, HEAD, d), lambda bi, s: (bi, jnp.minimum(s, steps - 1), 0, 0))
    st = pl.BlockSpec((nb, HEAD, d), lambda bi, s: (bi, 0, 0))
    old = pl.BlockSpec((nb, tc * CHUNK, d), prv)
    vec = pl.BlockSpec((1, d), lambda bi, s: (0, 0))
    return pl.pallas_call(
        functools.partial(_wkv_tail_kernel, nb=nb, nchunk=tc, ngroup=d // GW, alpha=alpha),
        grid=(bsz // nb, steps + 1),
        in_specs=[seq, seq, mat, mat, st, old, old, old,
                  pl.BlockSpec((d, d), lambda bi, s: (0, 0), pipeline_mode=pl.Buffered(1)),
                  vec, vec, vec, vec],
        out_specs=[old, st],
        out_shape=[jax.ShapeDtypeStruct((bsz, t, d), F32), jax.ShapeDtypeStruct((bsz, HEAD, d), F32)],
        scratch_shapes=[pltpu.VMEM((nb, HEAD, d), F32), pltpu.VMEM((nb * tc * CHUNK, d), F32)],
        compiler_params=_cparams(("parallel", "arbitrary")),
        name="wkv_tail",
    )(qp, y0, m, dmat, h0, bonus, g, x, w_o, gn_g, gn_b, ln_g, ln_b)


def _conv_kernel(xc_ref, xp_ref, st_ref, win_w, bin_ref, dw_ref, dwb_ref, cg_ref, cb_ref, wo_ref, bo_ref,
                 lng_ref, lnb_ref, o_ref, st_o, win_ref, u_ref, c_ref, *, alpha, tm, tn, skew):
    step = pl.program_id(1)
    d = xc_ref.shape[2]

    @pl.when(step == 0)
    def _():
        win_ref[...] = jnp.zeros_like(win_ref)

    def conv_half():
        zrows = tm + 8
        for ls in range(0, d, GW):
            lanes = slice(ls, ls + GW)
            acc = jnp.zeros((tm, GW), F32)
            for r in range(8):
                z = None
                for q in range((HALO + 8) // 8):
                    j = 8 * q + r - 2
                    if j < 0 or j >= CONV_W:
                        continue
                    nrows = tm if r == 0 else zrows
                    term = win_ref[8 * q:8 * q + nrows, lanes] * dw_ref[j:j + 1, lanes]
                    z = term if z is None else z + term
                if r == 0:
                    acc = acc + z
                else:
                    acc = acc + pltpu.roll(z, zrows - r, axis=0)[:tm]
            c_ref[:, lanes] = acc

        c = _layer_norm(c_ref[...] + dwb_ref[...], cg_ref[...], cb_ref[...])
        c = (c * _sigmoid(c)).astype(BF16)
        out = _dot(c, wo_ref[...]) + bo_ref[...]
        o_ref[0] = _layer_norm(alpha * xp_ref[0] + out, lng_ref[...], lnb_ref[...])

    def glu_half():
        xb = _b16(xc_ref[0])
        for cs in range(0, d, tn):
            val = _dot(xb, win_w[:, cs:cs + tn]) + bin_ref[:, cs:cs + tn]
            gate = _dot(xb, win_w[:, d + cs:d + cs + tn]) + bin_ref[:, d + cs:d + cs + tn]
            u_ref[:, cs:cs + tn] = val * _sigmoid(gate)
        win_ref[0:HALO, :] = jnp.where(step == 0, st_ref[0], win_ref[tm:tm + HALO, :])
        win_ref[HALO:HALO + tm, :] = u_ref[...]

    if skew:
        conv_half()
        glu_half()
    else:
        glu_half()
        conv_half()

    @pl.when(step == pl.num_programs(1) - (2 if skew else 1))
    def _():
        st_o[0] = u_ref[tm - HALO:tm, :]


def _conv(x, state_pad, w_in, b_in, dw, dw_b, cln_g, cln_b, w_out, b_out, ln_g, ln_b, alpha, tm, tn):
    bsz, t, d = x.shape
    steps = t // tm
    skew = steps > 1
    vec = pl.BlockSpec((1, d), lambda b, s: (0, 0))
    once = lambda shape: pl.BlockSpec(shape, lambda b, s: (0, 0), pipeline_mode=pl.Buffered(1))
    hist = pl.BlockSpec((1, HALO, d), lambda b, s: (b, 0, 0))
    if skew:
        glu_tile = pl.BlockSpec((1, tm, d), lambda b, s: (b, jnp.minimum(s, steps - 1), 0))
        prev_tile = pl.BlockSpec((1, tm, d), lambda b, s: (b, jnp.maximum(s - 1, 0), 0))
    else:
        glu_tile = prev_tile = pl.BlockSpec((1, tm, d), lambda b, s: (b, s, 0))
    return pl.pallas_call(
        functools.partial(_conv_kernel, alpha=alpha, tm=tm, tn=tn, skew=skew),
        grid=(bsz, steps + 1 if skew else steps),
        in_specs=[
            glu_tile,
            prev_tile, hist,
            once((d, 2 * d)), pl.BlockSpec((1, 2 * d), lambda b, s: (0, 0)),
            pl.BlockSpec((HALO, d), lambda b, s: (0, 0)),
            vec, vec, vec,
            once((d, d)),
            vec, vec, vec,
        ],
        out_specs=[prev_tile, hist],
        out_shape=[jax.ShapeDtypeStruct((bsz, t, d), F32), jax.ShapeDtypeStruct((bsz, HALO, d), F32)],
        scratch_shapes=[pltpu.VMEM((HALO + tm, d), F32), pltpu.VMEM((tm, d), F32), pltpu.VMEM((tm, d), F32)],
        compiler_params=_cparams(("parallel", "arbitrary")),
        name="conv",
    )(x, x, state_pad, w_in, b_in, dw, dw_b, cln_g, cln_b, w_out, b_out, ln_g, ln_b)


def _pad_cols(w, n):
    return jnp.pad(w, ((0, 0), (0, n - w.shape[1])))


def _pad_rows(w, n):
    return jnp.pad(w, ((0, n - w.shape[0]), (0, 0)))


def _round_up(n, m):
    return -(-n // m) * m


def _prep_params(raw):
    depth = raw['w_up'].shape[0]
    n_rwkv = raw['w_r'].shape[0]
    n_conv = raw['c_w_in'].shape[0]
    row = lambda v: v.reshape(1, -1)
    rwkv = []
    for j in range(n_rwkv):
        lw = _round_up(raw['w1'].shape[2], 128)
        la = _round_up(raw['a1'].shape[2], 128)
        rwkv.append(dict(
            mu=_pad_rows(raw['mu'][j], 8),
            w_r=raw['w_r'][j].astype(BF16), w_k=raw['w_k'][j].astype(BF16), w_v=raw['w_v'][j].astype(BF16),
            w_o=raw['w_o'][j].astype(BF16),
            w1=_pad_cols(raw['w1'][j], lw).astype(BF16), w2=_pad_rows(raw['w2'][j], lw).astype(BF16),
            a1=_pad_cols(raw['a1'][j], la).astype(BF16), a2=_pad_rows(raw['a2'][j], la).astype(BF16),
            g1=raw['g1'][j].astype(BF16), g2=raw['g2'][j].astype(BF16),
            w0=row(raw['w0'][j]), a0=row(raw['a0'][j]), k_k=row(raw['k_k'][j]), k_a=row(raw['k_a'][j]),
            r_k=row(raw['r_k'][j]), gn_g=row(raw['gn_g'][j]), gn_b=row(raw['gn_b'][j])))
    conv = []
    for j in range(n_conv):
        conv.append(dict(
            w_in=raw['c_w_in'][j].astype(BF16), b_in=row(raw['c_b_in'][j]),
            dw=_pad_rows(raw['c_dw'][j], HALO), dw_b=row(raw['c_dw_b'][j]),
            ln_g=row(raw['c_ln_g'][j]), ln_b=row(raw['c_ln_b'][j]),
            w_out=raw['c_w_out'][j].astype(BF16), b_out=row(raw['c_b_out'][j])))
    w_up, w_down = raw['w_up'].astype(BF16), raw['w_down'].astype(BF16)
    mlp = [dict(w_up=w_up, w_down=w_down,
                mix_g=row(raw['ln_mix_g'][i]), mix_b=row(raw['ln_mix_b'][i]),
                ffn_g=row(raw['ln_ffn_g'][i]), ffn_b=row(raw['ln_ffn_b'][i])) for i in range(depth)]
    return rwkv, conv, mlp


def _rwkv_layer(x, shift, wkv, p, lnp, alpha, cfg):
    bsz, t, d = x.shape
    pack = cfg['seq_pack']
    if pack == 1:
        *proj, last = _rwkv_proj(x, shift, p, cfg['tm_seq'], cfg['tn'])
        new_shift = last[:, SUBLANE - 1]
    else:
        packed = lambda a: a.reshape(bsz // pack, pack * t, d)
        first_rows = jnp.zeros_like(x).at[:, 0].set(shift)
        *proj, _ = _rwkv_proj(packed(x), packed(first_rows), p, pack * t, cfg['tn'], seq_rows=t)
        proj = [a.reshape(bsz, t, d) for a in proj]
        new_shift = x[:, -1]
    r, lw, k, v, kap, b, g, bonus = proj
    tp = _round_up(t, CHUNK)
    xp = x
    if tp != t:
        pad = lambda a: jnp.pad(a, ((0, 0), (0, tp - t), (0, 0)))
        r, lw, k, v, kap, b, g, bonus, xp = [pad(a) for a in (r, lw, k, v, kap, b, g, bonus, x)]
    qp, y0, m, dmat = _wkv_pre(r, lw, k, v, kap, b, cfg['tc_pre'], cfg['tg_pre'])
    nh = d // HEAD
    h0 = jnp.transpose(wkv, (0, 3, 1, 2)).reshape(bsz, HEAD, d)
    x1, h1 = _wkv_tail(qp, y0, m, dmat, h0, bonus, g, xp, p['w_o'], p['gn_g'], p['gn_b'],
                       lnp['mix_g'], lnp['mix_b'], alpha, cfg['nb_seq'], cfg['tc_seq'])
    new_wkv = jnp.transpose(h1.reshape(bsz, HEAD, nh, HEAD), (0, 2, 3, 1))
    return x1[:, :t], new_shift, new_wkv


def _conv_layer(x, state, p, lnp, alpha, cfg):
    bsz, t, d = x.shape
    assert t >= HALO
    state_pad = jnp.pad(state, ((0, 0), (HALO - CONV_STATE, 0), (0, 0)))
    x1, hist = _conv(x, state_pad, p['w_in'], p['b_in'], p['dw'], p['dw_b'], p['ln_g'], p['ln_b'],
                     p['w_out'], p['b_out'], lnp['mix_g'], lnp['mix_b'], alpha, cfg['tm_conv'], cfg['tn'])
    return x1, hist[:, HALO - CONV_STATE:]


def _run_trunk(x, st_shift, st_wkv, st_conv, params, cfg):
    rwkv, conv, mlp = params
    depth = len(mlp)
    alpha = (2.0 * depth) ** 0.25
    bsz, t, d = x.shape
    new_shift, new_wkv, new_conv = [], [], []
    for i in range(depth):
        j = i // 2
        if i % 2 == 0:
            x, sh, s = _rwkv_layer(x, st_shift[j], st_wkv[j], rwkv[j], mlp[i], alpha, cfg)
            new_shift.append(sh)
            new_wkv.append(s)
        else:
            x, cs = _conv_layer(x, st_conv[j], conv[j], mlp[i], alpha, cfg)
            new_conv.append(cs)
        x = _mlp(x.reshape(bsz * t, d), mlp[i]['w_up'], mlp[i]['w_down'], i, mlp[i]['ffn_g'], mlp[i]['ffn_b'],
                 alpha, cfg['tm_rows'], cfg['tf']).reshape(bsz, t, d)
    return x, jnp.stack(new_wkv), jnp.stack(new_shift), jnp.stack(new_conv)


def _config(bsz, t, d):
    rows = bsz * t
    tm_seq = min(t, 512)
    nchunk = _round_up(t, CHUNK) // CHUNK
    tc_pre = min(nchunk, WKV_UNITS)
    tg_pre = min(WKV_UNITS // tc_pre, d // GW)
    seq_pack = math.gcd(bsz, max(1, 512 // t))
    return dict(
        tm_seq=tm_seq, seq_pack=seq_pack, tn=512, tm_rows=min(rows, 512), tf=1024,
        tm_conv=min(t, 256), tc_pre=tc_pre, tg_pre=tg_pre, tc_seq=min(nchunk, 2), nb_seq=min(bsz, 2))


def kernel(x_prompt, x_sample, state_wkv, state_shift, state_conv, mu, w_r, w_k, w_v, w_o, w0, w1, w2, a0, a1, a2, g1, g2, k_k, k_a, r_k, gn_g, gn_b, c_w_in, c_b_in, c_dw, c_dw_b, c_ln_g, c_ln_b, c_w_out, c_b_out, w_up, w_down, ln_mix_g, ln_mix_b, ln_ffn_g, ln_ffn_b):
    raw = dict(mu=mu, w_r=w_r, w_k=w_k, w_v=w_v, w_o=w_o, w0=w0, w1=w1, w2=w2, a0=a0, a1=a1, a2=a2,
               g1=g1, g2=g2, k_k=k_k, k_a=k_a, r_k=r_k, gn_g=gn_g, gn_b=gn_b, c_w_in=c_w_in,
               c_b_in=c_b_in, c_dw=c_dw, c_dw_b=c_dw_b, c_ln_g=c_ln_g, c_ln_b=c_ln_b, c_w_out=c_w_out,
               c_b_out=c_b_out, w_up=w_up, w_down=w_down, ln_mix_g=ln_mix_g, ln_mix_b=ln_mix_b,
               ln_ffn_g=ln_ffn_g, ln_ffn_b=ln_ffn_b)
    params = _prep_params(raw)
    bp, tp, d = x_prompt.shape
    bs, ts, _ = x_sample.shape
    n_rwkv, n_conv = w_r.shape[0], c_w_in.shape[0]
    nh = d // HEAD
    z_wkv = jnp.zeros((n_rwkv, bp, nh, HEAD, HEAD), F32)
    z_shift = jnp.zeros((n_rwkv, bp, d), F32)
    z_conv = jnp.zeros((n_conv, bp, CONV_STATE, d), F32)
    y_p, wkv_p, shift_p, conv_p = _run_trunk(x_prompt, z_shift, z_wkv, z_conv, params, _config(bp, tp, d))
    y_s, wkv_s, shift_s, conv_s = _run_trunk(x_sample, state_shift, state_wkv, state_conv, params,
                                             _config(bs, ts, d))
    return (y_p, y_s, wkv_p, shift_p, conv_p, wkv_s, shift_s, conv_s)
```

```python
import functools
import math

import jax
import jax.numpy as jnp
from jax import lax
from jax.experimental import pallas as pl
from jax.experimental.pallas import tpu as pltpu

F32 = jnp.float32
BF16 = jnp.bfloat16

SUBLANE = 8
HEAD = 64
GROUP = 4
GW = HEAD * GROUP
CHUNK = 64
WKV_UNITS = 16
CONV_W = 31
CONV_STATE = CONV_W - 1
HALO = 32
LN_EPS = 1e-5
GN_EPS = 64e-5
VMEM_LIMIT = 56 * 1024 * 1024

NN = ((1,), (0,))
NT = ((1,), (1,))


def _cparams(sem):
    return pltpu.CompilerParams(dimension_semantics=sem, vmem_limit_bytes=VMEM_LIMIT)


def _dot(a, b, dims=NN):
    return lax.dot_general(a, b, (dims, ((), ())), preferred_element_type=F32)


def _split(x):
    hi = x.astype(BF16)
    lo = (x - hi.astype(F32)).astype(BF16)
    return hi, lo


def _layer_norm(x, g, b):
    mu = jnp.mean(x, axis=-1, keepdims=True)
    xc = x - mu
    var = jnp.mean(xc * xc, axis=-1, keepdims=True)
    return xc * lax.rsqrt(var + LN_EPS) * g + b


def _sigmoid(x):
    return 1.0 / (1.0 + jnp.exp(-x))


def _lane_head(shape, dim):
    return lax.broadcasted_iota(jnp.int32, shape, dim) // HEAD


def _block_diag(x):
    head = _lane_head(x.shape, 1)
    return jnp.concatenate([jnp.where(head == h, x, 0.0) for h in range(GROUP)], axis=0)


def _head_ones():
    return jnp.where(_lane_head((GW, GW), 0) == _lane_head((GW, GW), 1), 1.0, 0.0).astype(BF16)


def _head_sum(x, ones):
    hi, lo = _split(x)
    return _dot(hi, ones) + _dot(lo, ones)


def _mlp_kernel(x_ref, wu_ref, wd_ref, g_ref, b_ref, o_ref, acc_ref, xb_ref, *, alpha):
    j = pl.program_id(1)

    @pl.when(j == 0)
    def _():
        xb_ref[...] = x_ref[...].astype(BF16)
        acc_ref[...] = jnp.zeros_like(acc_ref)

    h = jnp.maximum(_dot(xb_ref[...], wu_ref[...]), 0.0)
    acc_ref[...] += _dot((h * h).astype(BF16), wd_ref[...])

    @pl.when(j == pl.num_programs(1) - 1)
    def _():
        o_ref[...] = _layer_norm(alpha * x_ref[...] + acc_ref[...], g_ref[...], b_ref[...])


def _mlp(x, w_up, w_down, layer, g, b, alpha, tm, tf):
    rows, d = x.shape
    dff = w_up.shape[2]
    return pl.pallas_call(
        functools.partial(_mlp_kernel, alpha=alpha),
        grid=(rows // tm, dff // tf),
        in_specs=[
            pl.BlockSpec((tm, d), lambda i, j: (i, 0)),
            pl.BlockSpec((None, d, tf), lambda i, j: (layer, 0, j)),
            pl.BlockSpec((None, tf, d), lambda i, j: (layer, j, 0)),
            pl.BlockSpec((1, d), lambda i, j: (0, 0)),
            pl.BlockSpec((1, d), lambda i, j: (0, 0)),
        ],
        out_specs=pl.BlockSpec((tm, d), lambda i, j: (i, 0)),
        out_shape=jax.ShapeDtypeStruct((rows, d), F32),
        scratch_shapes=[pltpu.VMEM((tm, d), F32), pltpu.VMEM((tm, d), BF16)],
        compiler_params=_cparams(("parallel", "arbitrary")),
        name="mlp",
    )(x, w_up, w_down, g, b)


def _rwkv_proj_kernel(x_ref, tail_ref, shift_ref, mu_ref, w1_ref, a1_ref, g1_ref,
                      wr_ref, wk_ref, wv_ref, w2_ref, a2_ref, g2_ref,
                      w0_ref, a0_ref, kk_ref, ka_ref, rk_ref,
                      r_o, lw_o, k_o, v_o, kap_o, b_o, g_o, bonus_o, last_o,
                      xm_ref, hw_ref, ha_ref, hg_ref, *, seq_rows):
    j = pl.program_id(2)

    @pl.when(j == 0)
    def _():
        x = x_ref[0]
        rolled = pltpu.roll(x, 1, axis=0)
        row = lax.broadcasted_iota(jnp.int32, x.shape, 0)
        if seq_rows is None:
            first = row == 0
            before = jnp.where(pl.program_id(1) == 0, shift_ref[0], tail_ref[0, SUBLANE - 1:SUBLANE, :])
        else:
            first = row % seq_rows == 0
            before = shift_ref[0]
        xx = jnp.where(first, before, rolled) - x
        last_o[0] = x[x.shape[0] - SUBLANE:, :]
        x16, xx16, mu = _b16(x), _b16(xx), _b16(mu_ref[...])
        xm_ref[0] = x16 + xx16 * mu[0:1]
        xm_ref[1] = x16 + xx16 * mu[2:3]
        xm_ref[2] = x16 + xx16 * mu[3:4]
        xw = x16 + xx16 * mu[1:2]
        xa = x16 + xx16 * mu[4:5]
        xg = x16 + xx16 * mu[5:6]
        hw_ref[...] = jnp.tanh(_dot(xw, w1_ref[...])).astype(BF16)
        ha_ref[...] = _dot(xa, a1_ref[...]).astype(BF16)
        hg_ref[...] = _sigmoid(_dot(xg, g1_ref[...])).astype(BF16)

    r = _dot(xm_ref[0], wr_ref[...])
    k = _dot(xm_ref[1], wk_ref[...])
    v = _dot(xm_ref[2], wv_ref[...])
    z = -(w0_ref[...] + _dot(hw_ref[...], w2_ref[...]))
    w_log = -(jnp.maximum(z, 0.0) + jnp.log(1.0 + jnp.exp(-jnp.abs(z)))) - 0.5
    a = _sigmoid(a0_ref[...] + _dot(ha_ref[...], a2_ref[...]))
    g = _dot(hg_ref[...], g2_ref[...])

    kk = k * kk_ref[...]
    ones = _head_ones()
    slabs = range(0, kk.shape[1], GW)
    sq = kk * kk
    ss = jnp.concatenate([_head_sum(sq[:, s:s + GW], ones) for s in slabs], axis=1)
    kap = kk * lax.rsqrt(jnp.maximum(ss, 1e-24))
    k2 = k * (1.0 + (a - 1.0) * ka_ref[...])
    rkk = r * k2 * rk_ref[...]
    bonus = jnp.concatenate([_dot(_b16(rkk[:, s:s + GW]), ones) for s in slabs], axis=1) * v

    r_o[0] = r
    lw_o[0] = -jnp.exp(w_log)
    k_o[0] = k2
    v_o[0] = v
    kap_o[0] = kap
    b_o[0] = kap * a
    g_o[0] = g
    bonus_o[0] = bonus


def _rwkv_proj(x, shift, p, tm, tn, seq_rows=None):
    bsz, t, d = x.shape
    nt = t // tm
    tail_blocks = tm // SUBLANE
    if seq_rows is None:
        shift_arr, shift_spec = shift[:, None, :], pl.BlockSpec((1, 1, d), lambda b, i, j: (b, 0, 0))
    else:
        shift_arr, shift_spec = shift, pl.BlockSpec((1, tm, d), lambda b, i, j: (b, i, 0))
    lw, la, lg = p['w1'].shape[1], p['a1'].shape[1], p['g1'].shape[1]
    full = lambda shape: pl.BlockSpec(shape, lambda b, i, j: (0,) * len(shape))
    col = lambda rows: pl.BlockSpec((rows, tn), lambda b, i, j: (0, j))
    out_spec = pl.BlockSpec((1, tm, tn), lambda b, i, j: (b, i, j))
    out_sds = jax.ShapeDtypeStruct((bsz, t, d), F32)
    return pl.pallas_call(
        functools.partial(_rwkv_proj_kernel, seq_rows=seq_rows),
        grid=(bsz, nt, d // tn),
        in_specs=[
            pl.BlockSpec((1, tm, d), lambda b, i, j: (b, i, 0)),
            pl.BlockSpec((1, SUBLANE, d), lambda b, i, j: (b, jnp.maximum(i * tail_blocks - 1, 0), 0)),
            shift_spec,
            full((8, d)), full((d, lw)), full((d, la)), full((d, lg)),
            col(d), col(d), col(d), col(lw), col(la), col(lg),
            col(1), col(1), col(1), col(1), col(1),
        ],
        out_specs=[out_spec] * 8 + [pl.BlockSpec((1, SUBLANE, d), lambda b, i, j: (b, 0, 0))],
        out_shape=[out_sds] * 8 + [jax.ShapeDtypeStruct((bsz, SUBLANE, d), F32)],
        scratch_shapes=[pltpu.VMEM((3, tm, d), BF16), pltpu.VMEM((tm, lw), BF16),
                        pltpu.VMEM((tm, la), BF16), pltpu.VMEM((tm, lg), BF16)],
        compiler_params=_cparams(("parallel", "arbitrary", "arbitrary")),
        name="rwkv_proj",
    )(x, x, shift_arr, p['mu'], p['w1'], p['a1'], p['g1'], p['w_r'], p['w_k'], p['w_v'],
      p['w2'], p['a2'], p['g2'], p['w0'], p['a0'], p['k_k'], p['k_a'], p['r_k'])


def _head_transpose(x):
    xt = x.T
    return jnp.concatenate([xt[h * HEAD:(h + 1) * HEAD, :] for h in range(GROUP)], axis=1)


def _b16(x):
    return x.astype(BF16)


def _each(f, *lists):
    return [f(*args) for args in zip(*lists)]


def _bd16(x):
    return _b16(_block_diag(x))


def _wkv_chunks(r, lw, k, v, kap, b):
    c = CHUNK
    row = lax.broadcasted_iota(jnp.int32, (c, GW), 0)
    pos = lax.broadcasted_iota(jnp.int32, (c, GW), 1) % HEAD
    tri = jnp.where(lax.broadcasted_iota(jnp.int32, (c, c), 1) <= lax.broadcasted_iota(jnp.int32, (c, c), 0),
                    1.0, 0.0).astype(BF16)
    strict = pos < row
    lower = pos <= row
    diag = pos == row

    def cumsum(x):
        hi = _b16(x)
        rem = x - hi.astype(F32)
        mid = _b16(rem)
        lo = _b16(rem - mid.astype(F32))
        return _dot(tri, hi) + (_dot(tri, mid) + _dot(tri, lo))

    cum = _each(cumsum, lw)
    end = _each(lambda s: s[c - 1:c, :], cum)
    e_neg = _each(lambda s: jnp.exp(-s), cum)
    e_end = _each(lambda e, s: jnp.exp(e - s), end, cum)
    rt = _each(lambda x, s: x * jnp.exp(s), r, cum)
    kq = _each(lambda x, s, l: x * jnp.exp(s - l), kap, cum, lw)
    bh = _each(jnp.multiply, b, e_neg)
    kh = _each(jnp.multiply, k, e_neg)
    bp = _each(jnp.multiply, b, e_end)
    kp = _each(jnp.multiply, k, e_end)

    lhs = _each(lambda x, y: _b16(jnp.concatenate([x, y], axis=0)), kq, rt)
    gb = _each(lambda x, y: _dot(x, _bd16(y), NT), lhs, bh)
    gk = _each(lambda x, y: _dot(x, _bd16(y), NT), lhs, kh)
    l_ub = _each(lambda g: jnp.where(strict, g[:c], 0.0), gb)
    a_uk = _each(lambda g: jnp.where(strict, g[:c], 0.0), gk)
    a_rb = _each(lambda g: jnp.where(lower, g[c:], 0.0), gb)
    a_rk = _each(lambda g: jnp.where(lower, g[c:], 0.0), gk)

    stack = lambda x, y: _b16(jnp.concatenate([x, y], axis=0))
    nm = _each(jnp.negative, l_ub)
    pw = _each(lambda x: _dot(_b16(x), _bd16(x)), l_ub)
    for _ in range(int(math.log2(c)) - 2):
        res = _each(lambda x, n: _dot(stack(x, n), _bd16(x)), pw, nm)
        nm = _each(lambda n, x, y: n + x + y[c:], nm, pw, res)
        pw = _each(lambda y: y[:c], res)
    nm = _each(lambda n, x: n + x + _dot(_b16(n), _bd16(x)), nm, pw)
    nm16 = _each(_b16, nm)

    bpx = _each(lambda x: _b16(_head_transpose(x)), bp)
    kpx = _each(lambda x: _b16(_head_transpose(x)), kp)
    wa = _each(lambda x, y, z, w: _dot(jnp.concatenate([stack(x, y), z], axis=0), _bd16(w)), a_uk, a_rk, kpx, v)
    p = _each(lambda x, n: x + _dot(n, _bd16(x)), kq, nm16)
    u0 = _each(lambda x, n: -(x[:c] + _dot(n, _bd16(x[:c]))), wa, nm16)

    lhs2 = _each(lambda a, x: jnp.concatenate([_b16(a), x], axis=0), a_rb, bpx)
    rp = _each(lambda a, y: _dot(a, _bd16(y)), lhs2, p)
    ru = _each(lambda a, y: _dot(a, _bd16(y)), lhs2, u0)
    qp = _each(lambda x, y: x - y[:c], rt, rp)
    y0 = _each(lambda y, x: y[:c] + x[c:2 * c], ru, wa)
    m_cat = _each(lambda e, y: jnp.where(diag, jnp.exp(e), 0.0) - y[c:], end, rp)
    d_cat = _each(lambda y, x: y[c:] + x[2 * c:], ru, wa)
    return qp, y0, m_cat, d_cat


def _wkv_pre_kernel(r_ref, lw_ref, k_ref, v_ref, kap_ref, b_ref, qp_o, y0_o, m_o, d_o, *, nchunk, ngroup):
    units = [(ci, slice(ci * CHUNK, (ci + 1) * CHUNK), slice(gi * GW, (gi + 1) * GW))
             for ci in range(nchunk) for gi in range(ngroup)]
    load = lambda ref: [ref[0, rs, ls] for _, rs, ls in units]
    qp, y0, m_cat, d_cat = _wkv_chunks(load(r_ref), load(lw_ref), load(k_ref), load(v_ref),
                                       load(kap_ref), load(b_ref))
    for n, (ci, rs, ls) in enumerate(units):
        qp_o[0, rs, ls] = qp[n]
        y0_o[0, rs, ls] = y0[n]
        m_o[0, ci, :, ls] = m_cat[n]
        d_o[0, ci, :, ls] = d_cat[n]


def _wkv_pre(r, lw, k, v, kap, b, tc, tg):
    bsz, t, d = r.shape
    nc = t // CHUNK
    seq = pl.BlockSpec((1, tc * CHUNK, tg * GW), lambda bi, ci, gi: (bi, ci, gi))
    mat = pl.BlockSpec((1, tc, HEAD, tg * GW), lambda bi, ci, gi: (bi, ci, 0, gi))
    seq_sds = jax.ShapeDtypeStruct((bsz, t, d), F32)
    mat_sds = jax.ShapeDtypeStruct((bsz, nc, HEAD, d), F32)
    return pl.pallas_call(
        functools.partial(_wkv_pre_kernel, nchunk=tc, ngroup=tg),
        grid=(bsz, nc // tc, d // (tg * GW)),
        in_specs=[seq] * 6,
        out_specs=[seq, seq, mat, mat],
        out_shape=[seq_sds, seq_sds, mat_sds, mat_sds],
        compiler_params=_cparams(("parallel", "parallel", "parallel")),
        name="wkv_pre",
    )(r, lw, k, v, kap, b)


def _wkv_tail_kernel(qp_ref, y0_ref, m_ref, d_ref, h0_ref, bonus_ref, g_ref, x_ref, wo_ref,
                     gng_ref, gnb_ref, lng_ref, lnb_ref, o_ref, h_o, h_ref, y_ref,
                     *, nb, nchunk, ngroup, alpha):
    s = pl.program_id(1)
    rows_b = nchunk * CHUNK

    @pl.when(s == 0)
    def _():
        h_ref[...] = h0_ref[...]
        y_ref[...] = jnp.zeros_like(y_ref)

    ones = _head_ones()
    y = y_ref[...]
    parts = []
    for ls in range(0, y.shape[1], GW):
        ys = y[:, ls:ls + GW]
        mean = _dot(_b16(ys), ones) * (1.0 / HEAD)
        yc = ys - mean
        var = _dot(_b16(yc * yc), ones) * (1.0 / HEAD)
        parts.append(yc * lax.rsqrt(var + GN_EPS))
    yn = jnp.concatenate(parts, axis=1) * gng_ref[...] + gnb_ref[...]
    rows_of = lambda ref: jnp.concatenate([ref[bi] for bi in range(nb)], axis=0)
    z = _b16((yn + rows_of(bonus_ref)) * rows_of(g_ref))
    res = _layer_norm(alpha * rows_of(x_ref) + _dot(z, wo_ref[...]), lng_ref[...], lnb_ref[...])
    for bi in range(nb):
        o_ref[bi] = res[bi * rows_b:(bi + 1) * rows_b]

    for c in range(nchunk):
        rows = slice(c * CHUNK, (c + 1) * CHUNK)
        for bi in range(nb):
            for gi in range(ngroup):
                lanes = slice(gi * GW, (gi + 1) * GW)
                h_hi, h_lo = _split(_block_diag(h_ref[bi, :, lanes]))
                m_hi, m_lo = _split(m_ref[bi, c, :, lanes])
                lhs = jnp.concatenate([_b16(qp_ref[bi, rows, lanes]), m_hi, m_lo], axis=0)
                out = _dot(lhs, h_hi)
                y_ref[bi * rows_b + c * CHUNK:bi * rows_b + (c + 1) * CHUNK, lanes] = (
                    out[:CHUNK] + y0_ref[bi, rows, lanes])
                h_ref[bi, :, lanes] = (out[CHUNK:2 * CHUNK] + (out[2 * CHUNK:] + _dot(m_hi, h_lo))
                                       + d_ref[bi, c, :, lanes])

    @pl.when(s == pl.num_programs(1) - 2)
    def _():
        h_o[...] = h_ref[...]


def _wkv_tail(qp, y0, m, dmat, h0, bonus, g, x, w_o, gn_g, gn_b, ln_g, ln_b, alpha, nb, tc):
    bsz, t, d = qp.shape
    steps = t // (tc * CHUNK)
    cur = lambda bi, s: (bi, jnp.minimum(s, steps - 1), 0)
    prv = lambda bi, s: (bi, jnp.maximum(s - 1, 0), 0)
    seq = pl.BlockSpec((nb, tc * CHUNK, d), cur)
    mat = pl.BlockSpec((nb, tc, HEAD, d), lambda bi, s: (bi, jnp.minimum(s, steps - 1), 0, 0))
    st = pl.BlockSpec((nb, HEAD, d), lambda bi, s: (bi, 0, 0))
    old = pl.BlockSpec((nb, tc * CHUNK, d), prv)
    vec = pl.BlockSpec((1, d), lambda bi, s: (0, 0))
    return pl.pallas_call(
        functools.partial(_wkv_tail_kernel, nb=nb, nchunk=tc, ngroup=d // GW, alpha=alpha),
        grid=(bsz // nb, steps + 1),
        in_specs=[seq, seq, mat, mat, st, old, old, old,
                  pl.BlockSpec((d, d), lambda bi, s: (0, 0), pipeline_mode=pl.Buffered(1)),
                  vec, vec, vec, vec],
        out_specs=[old, st],
        out_shape=[jax.ShapeDtypeStruct((bsz, t, d), F32), jax.ShapeDtypeStruct((bsz, HEAD, d), F32)],
        scratch_shapes=[pltpu.VMEM((nb, HEAD, d), F32), pltpu.VMEM((nb * tc * CHUNK, d), F32)],
        compiler_params=_cparams(("parallel", "arbitrary")),
        name="wkv_tail",
    )(qp, y0, m, dmat, h0, bonus, g, x, w_o, gn_g, gn_b, ln_g, ln_b)


def _conv_kernel(xc_ref, xp_ref, st_ref, win_w, bin_ref, dw_ref, dwb_ref, cg_ref, cb_ref, wo_ref, bo_ref,
                 lng_ref, lnb_ref, o_ref, st_o, win_ref, u_ref, c_ref, *, alpha, tm, tn, skew):
    step = pl.program_id(1)
    d = xc_ref.shape[2]

    @pl.when(step == 0)
    def _():
        win_ref[...] = jnp.zeros_like(win_ref)

    def conv_half():
        zrows = tm + 8
        sub = lax.broadcasted_iota(jnp.int32, (tm, GW), 0) % SUBLANE
        for ls in range(0, d, GW):
            lanes = slice(ls, ls + GW)
            acc = jnp.zeros((tm, GW), F32)
            for r in range(8):
                z = None
                for q in range((HALO + 8) // 8):
                    j = 8 * q + r - 2
                    if j < 0 or j >= CONV_W:
                        continue
                    nrows = tm if r == 0 else zrows
                    term = win_ref[8 * q:8 * q + nrows, lanes] * dw_ref[j:j + 1, lanes]
                    z = term if z is None else z + term
                if r == 0:
                    acc = acc + z
                else:
                    m = jnp.where(sub >= r, z[:tm], z[SUBLANE:tm + SUBLANE])
                    m = pltpu.roll(m.reshape(tm // SUBLANE, SUBLANE, GW), SUBLANE - r, axis=1)
                    acc = acc + m.reshape(tm, GW)
            c_ref[:, lanes] = acc

        c = _layer_norm(c_ref[...] + dwb_ref[...], cg_ref[...], cb_ref[...])
        c = (c * _sigmoid(c)).astype(BF16)
        out = _dot(c, wo_ref[...]) + bo_ref[...]
        o_ref[0] = _layer_norm(alpha * xp_ref[0] + out, lng_ref[...], lnb_ref[...])

    def glu_half():
        xb = _b16(xc_ref[0])
        for cs in range(0, d, tn):
            val = _dot(xb, win_w[:, cs:cs + tn]) + bin_ref[:, cs:cs + tn]
            gate = _dot(xb, win_w[:, d + cs:d + cs + tn]) + bin_ref[:, d + cs:d + cs + tn]
            u_ref[:, cs:cs + tn] = val * _sigmoid(gate)
        win_ref[0:HALO, :] = jnp.where(step == 0, st_ref[0], win_ref[tm:tm + HALO, :])
        win_ref[HALO:HALO + tm, :] = u_ref[...]

    if skew:
        conv_half()
        glu_half()
    else:
        glu_half()
        conv_half()

    @pl.when(step == pl.num_programs(1) - (2 if skew else 1))
    def _():
        st_o[0] = u_ref[tm - HALO:tm, :]


def _conv(x, state_pad, w_in, b_in, dw, dw_b, cln_g, cln_b, w_out, b_out, ln_g, ln_b, alpha, tm, tn):
    bsz, t, d = x.shape
    steps = t // tm
    skew = steps > 1
    vec = pl.BlockSpec((1, d), lambda b, s: (0, 0))
    once = lambda shape: pl.BlockSpec(shape, lambda b, s: (0, 0), pipeline_mode=pl.Buffered(1))
    hist = pl.BlockSpec((1, HALO, d), lambda b, s: (b, 0, 0))
    if skew:
        glu_tile = pl.BlockSpec((1, tm, d), lambda b, s: (b, jnp.minimum(s, steps - 1), 0))
        prev_tile = pl.BlockSpec((1, tm, d), lambda b, s: (b, jnp.maximum(s - 1, 0), 0))
    else:
        glu_tile = prev_tile = pl.BlockSpec((1, tm, d), lambda b, s: (b, s, 0))
    return pl.pallas_call(
        functools.partial(_conv_kernel, alpha=alpha, tm=tm, tn=tn, skew=skew),
        grid=(bsz, steps + 1 if skew else steps),
        in_specs=[
            glu_tile,
            prev_tile, hist,
            once((d, 2 * d)), pl.BlockSpec((1, 2 * d), lambda b, s: (0, 0)),
            pl.BlockSpec((HALO, d), lambda b, s: (0, 0)),
            vec, vec, vec,
            once((d, d)),
            vec, vec, vec,
        ],
        out_specs=[prev_tile, hist],
        out_shape=[jax.ShapeDtypeStruct((bsz, t, d), F32), jax.ShapeDtypeStruct((bsz, HALO, d), F32)],
        scratch_shapes=[pltpu.VMEM((HALO + tm, d), F32), pltpu.VMEM((tm, d), F32), pltpu.VMEM((tm, d), F32)],
        compiler_params=_cparams(("parallel", "arbitrary")),
        name="conv",
    )(x, x, state_pad, w_in, b_in, dw, dw_b, cln_g, cln_b, w_out, b_out, ln_g, ln_b)


def _pad_cols(w, n):
    return jnp.pad(w, ((0, 0), (0, n - w.shape[1])))


def _pad_rows(w, n):
    return jnp.pad(w, ((0, n - w.shape[0]), (0, 0)))


def _round_up(n, m):
    return -(-n // m) * m


def _prep_params(raw):
    depth = raw['w_up'].shape[0]
    n_rwkv = raw['w_r'].shape[0]
    n_conv = raw['c_w_in'].shape[0]
    row = lambda v: v.reshape(1, -1)
    rwkv = []
    for j in range(n_rwkv):
        lw = _round_up(raw['w1'].shape[2], 128)
        la = _round_up(raw['a1'].shape[2], 128)
        rwkv.append(dict(
            mu=_pad_rows(raw['mu'][j], 8),
            w_r=raw['w_r'][j].astype(BF16), w_k=raw['w_k'][j].astype(BF16), w_v=raw['w_v'][j].astype(BF16),
            w_o=raw['w_o'][j].astype(BF16),
            w1=_pad_cols(raw['w1'][j], lw).astype(BF16), w2=_pad_rows(raw['w2'][j], lw).astype(BF16),
            a1=_pad_cols(raw['a1'][j], la).astype(BF16), a2=_pad_rows(raw['a2'][j], la).astype(BF16),
            g1=raw['g1'][j].astype(BF16), g2=raw['g2'][j].astype(BF16),
            w0=row(raw['w0'][j]), a0=row(raw['a0'][j]), k_k=row(raw['k_k'][j]), k_a=row(raw['k_a'][j]),
            r_k=row(raw['r_k'][j]), gn_g=row(raw['gn_g'][j]), gn_b=row(raw['gn_b'][j])))
    conv = []
    for j in range(n_conv):
        conv.append(dict(
            w_in=raw['c_w_in'][j].astype(BF16), b_in=row(raw['c_b_in'][j]),
            dw=_pad_rows(raw['c_dw'][j], HALO), dw_b=row(raw['c_dw_b'][j]),
            ln_g=row(raw['c_ln_g'][j]), ln_b=row(raw['c_ln_b'][j]),
            w_out=raw['c_w_out'][j].astype(BF16), b_out=row(raw['c_b_out'][j])))
    w_up, w_down = raw['w_up'].astype(BF16), raw['w_down'].astype(BF16)
    mlp = [dict(w_up=w_up, w_down=w_down,
                mix_g=row(raw['ln_mix_g'][i]), mix_b=row(raw['ln_mix_b'][i]),
                ffn_g=row(raw['ln_ffn_g'][i]), ffn_b=row(raw['ln_ffn_b'][i])) for i in range(depth)]
    return rwkv, conv, mlp


def _rwkv_layer(x, shift, wkv, p, lnp, alpha, cfg):
    bsz, t, d = x.shape
    pack = cfg['seq_pack']
    if pack == 1:
        *proj, last = _rwkv_proj(x, shift, p, cfg['tm_seq'], cfg['tn'])
        new_shift = last[:, SUBLANE - 1]
    else:
        packed = lambda a: a.reshape(bsz // pack, pack * t, d)
        first_rows = jnp.zeros_like(x).at[:, 0].set(shift)
        *proj, _ = _rwkv_proj(packed(x), packed(first_rows), p, pack * t, cfg['tn'], seq_rows=t)
        proj = [a.reshape(bsz, t, d) for a in proj]
        new_shift = x[:, -1]
    r, lw, k, v, kap, b, g, bonus = proj
    tp = _round_up(t, CHUNK)
    xp = x
    if tp != t:
        pad = lambda a: jnp.pad(a, ((0, 0), (0, tp - t), (0, 0)))
        r, lw, k, v, kap, b, g, bonus, xp = [pad(a) for a in (r, lw, k, v, kap, b, g, bonus, x)]
    qp, y0, m, dmat = _wkv_pre(r, lw, k, v, kap, b, cfg['tc_pre'], cfg['tg_pre'])
    nh = d // HEAD
    h0 = jnp.transpose(wkv, (0, 3, 1, 2)).reshape(bsz, HEAD, d)
    x1, h1 = _wkv_tail(qp, y0, m, dmat, h0, bonus, g, xp, p['w_o'], p['gn_g'], p['gn_b'],
                       lnp['mix_g'], lnp['mix_b'], alpha, cfg['nb_seq'], cfg['tc_seq'])
    new_wkv = jnp.transpose(h1.reshape(bsz, HEAD, nh, HEAD), (0, 2, 3, 1))
    return x1[:, :t], new_shift, new_wkv


def _conv_layer(x, state, p, lnp, alpha, cfg):
    bsz, t, d = x.shape
    assert t >= HALO
    state_pad = jnp.pad(state, ((0, 0), (HALO - CONV_STATE, 0), (0, 0)))
    x1, hist = _conv(x, state_pad, p['w_in'], p['b_in'], p['dw'], p['dw_b'], p['ln_g'], p['ln_b'],
                     p['w_out'], p['b_out'], lnp['mix_g'], lnp['mix_b'], alpha, cfg['tm_conv'], cfg['tn'])
    return x1, hist[:, HALO - CONV_STATE:]


def _run_trunk(x, st_shift, st_wkv, st_conv, params, cfg):
    rwkv, conv, mlp = params
    depth = len(mlp)
    alpha = (2.0 * depth) ** 0.25
    bsz, t, d = x.shape
    new_shift, new_wkv, new_conv = [], [], []
    for i in range(depth):
        j = i // 2
        if i % 2 == 0:
            x, sh, s = _rwkv_layer(x, st_shift[j], st_wkv[j], rwkv[j], mlp[i], alpha, cfg)
            new_shift.append(sh)
            new_wkv.append(s)
        else:
            x, cs = _conv_layer(x, st_conv[j], conv[j], mlp[i], alpha, cfg)
            new_conv.append(cs)
        x = _mlp(x.reshape(bsz * t, d), mlp[i]['w_up'], mlp[i]['w_down'], i, mlp[i]['ffn_g'], mlp[i]['ffn_b'],
                 alpha, cfg['tm_rows'], cfg['tf']).reshape(bsz, t, d)
    return x, jnp.stack(new_wkv), jnp.stack(new_shift), jnp.stack(new_conv)


def _config(bsz, t, d):
    rows = bsz * t
    tm_seq = min(t, 512)
    nchunk = _round_up(t, CHUNK) // CHUNK
    tc_pre = min(nchunk, WKV_UNITS)
    tg_pre = min(WKV_UNITS // tc_pre, d // GW)
    seq_pack = math.gcd(bsz, max(1, 512 // t))
    return dict(
        tm_seq=tm_seq, seq_pack=seq_pack, tn=512, tm_rows=min(rows, 512), tf=1024,
        tm_conv=min(t, 256), tc_pre=tc_pre, tg_pre=tg_pre, tc_seq=min(nchunk, 2), nb_seq=min(bsz, 2))


def kernel(x_prompt, x_sample, state_wkv, state_shift, state_conv, mu, w_r, w_k, w_v, w_o, w0, w1, w2, a0, a1, a2, g1, g2, k_k, k_a, r_k, gn_g, gn_b, c_w_in, c_b_in, c_dw, c_dw_b, c_ln_g, c_ln_b, c_w_out, c_b_out, w_up, w_down, ln_mix_g, ln_mix_b, ln_ffn_g, ln_ffn_b):
    raw = dict(mu=mu, w_r=w_r, w_k=w_k, w_v=w_v, w_o=w_o, w0=w0, w1=w1, w2=w2, a0=a0, a1=a1, a2=a2,
               g1=g1, g2=g2, k_k=k_k, k_a=k_a, r_k=r_k, gn_g=gn_g, gn_b=gn_b, c_w_in=c_w_in,
               c_b_in=c_b_in, c_dw=c_dw, c_dw_b=c_dw_b, c_ln_g=c_ln_g, c_ln_b=c_ln_b, c_w_out=c_w_out,
               c_b_out=c_b_out, w_up=w_up, w_down=w_down, ln_mix_g=ln_mix_g, ln_mix_b=ln_mix_b,
               ln_ffn_g=ln_ffn_g, ln_ffn_b=ln_ffn_b)
    params = _prep_params(raw)
    bp, tp, d = x_prompt.shape
    bs, ts, _ = x_sample.shape
    n_rwkv, n_conv = w_r.shape[0], c_w_in.shape[0]
    nh = d // HEAD
    z_wkv = jnp.zeros((n_rwkv, bp, nh, HEAD, HEAD), F32)
    z_shift = jnp.zeros((n_rwkv, bp, d), F32)
    z_conv = jnp.zeros((n_conv, bp, CONV_STATE, d), F32)
    y_p, wkv_p, shift_p, conv_p = _run_trunk(x_prompt, z_shift, z_wkv, z_conv, params, _config(bp, tp, d))
    y_s, wkv_s, shift_s, conv_s = _run_trunk(x_sample, state_shift, state_wkv, state_conv, params,
                                             _config(bs, ts, d))
    return (y_p, y_s, wkv_p, shift_p, conv_p, wkv_s, shift_s, conv_s)
```
